```python
import jax
import jax.numpy as jnp
from jax import lax
import numpy as np

D_MODEL = 1024
BATCH = 2
SEQ = 8192
DEPTH = 2

CTX_LEN = 256
GRID_W = 64
HEAD_DIM = 64
BRANCH_W = 512
N_BRANCH = 3
NA_HEADS = BRANCH_W // HEAD_DIM
NA_WIN_R = 8
NA_WIN_C = 16
NA_QBLK = 16
NA_KBLK = 32
LRU_WIDTH = BRANCH_W
LRU_BLOCKS = 8
LRU_CONV = 4
LRU_C = 8.0
SW_HEADS = BRANCH_W // HEAD_DIM
SW_KV_HEADS = 2
SW_WINDOW = 128
SW_BLOCK = 128
ROPE_BASE = 10000.0
ROPE_AXIS_DIM = HEAD_DIM // 2
N_EXPERTS = 32
TOP_K = 4
D_EXPERT = 1024
SWIGLU_LIMIT = 7.0
SWIGLU_ALPHA = 1.702
RMS_EPS = 1e-6
NEG_INF = -1e30
PROJ_SPLITS = (BRANCH_W, BRANCH_W, BRANCH_W, LRU_WIDTH, LRU_WIDTH, BRANCH_W, SW_KV_HEADS * HEAD_DIM, SW_KV_HEADS * HEAD_DIM, N_BRANCH * D_MODEL)
PROJ_WIDTH = 4 * BRANCH_W + 2 * LRU_WIDTH + 2 * SW_KV_HEADS * HEAD_DIM + N_BRANCH * D_MODEL

kernel_name = 'hybrid_natten_rglru_swa_moe_diffusion_block'

F32 = jnp.float32


def rms_norm(x, g):
    x32 = x.astype(F32)
    y = x32 * lax.rsqrt(jnp.mean(x32 * x32, axis=-1, keepdims=True) + RMS_EPS)
    return (y * g.astype(F32)).astype(x.dtype)


def modulate(h, shift, scale):
    return h * (1.0 + scale) + shift


def split_cols(p, widths):
    idx = np.cumsum(widths)[:-1].tolist()
    return jnp.split(p, idx, axis=-1)


def to_heads(t, n_heads):
    return t.reshape(t.shape[:2] + (n_heads, HEAD_DIM))


def axial_rope_tables(n_tok):
    pos = jnp.arange(n_tok, dtype=jnp.int32)
    row = (pos // GRID_W).astype(F32)
    col = (pos % GRID_W).astype(F32)
    inv = ROPE_BASE ** (-jnp.arange(0, ROPE_AXIS_DIM, 2, dtype=F32) / ROPE_AXIS_DIM)
    ang = jnp.stack([row[:, None] * inv, col[:, None] * inv], axis=1)
    return jnp.cos(ang), jnp.sin(ang)


def apply_axial_rope(x, cos, sin):
    xs = x.astype(F32).reshape(x.shape[:-1] + (2, 2, ROPE_AXIS_DIM // 2))
    x1, x2 = xs[..., 0, :], xs[..., 1, :]
    cs, sn = cos[None, :, None], sin[None, :, None]
    out = jnp.stack([x1 * cs - x2 * sn, x2 * cs + x1 * sn], axis=-2)
    return out.reshape(x.shape).astype(x.dtype)


def neighbourhood_attention(q, k, v, kc, vc, rpb):
    B, T, H, d = q.shape
    rows = T // GRID_W
    kr = min(NA_WIN_R, rows)
    n_cb = GRID_W // NA_QBLK
    scale = d ** -0.5
    qcol = np.arange(GRID_W).reshape(n_cb, NA_QBLK)
    blk_start = np.clip(qcol[:, 0] - NA_WIN_C // 2, 0, GRID_W - NA_KBLK)
    kcol = blk_start[:, None] + np.arange(NA_KBLK)
    q_start = np.clip(qcol - NA_WIN_C // 2, 0, GRID_W - NA_WIN_C)
    col_mask = (kcol[:, None, :] >= q_start[:, :, None]) & (kcol[:, None, :] < q_start[:, :, None] + NA_WIN_C)
    col_mask_b = jnp.asarray(col_mask[:, :, None, :])
    col_off = jnp.asarray(np.clip(kcol[:, None, :] - qcol[:, :, None] + NA_WIN_C - 1, 0, 2 * NA_WIN_C - 2))
    qg = q.reshape(B, rows, n_cb, NA_QBLK, H, d)
    kg = k.reshape(B, rows, GRID_W, H, d)
    vg = v.reshape(B, rows, GRID_W, H, d)

    def row_block(r):
        rs = jnp.clip(r - NA_WIN_R // 2, 0, rows - kr)
        k_blk = lax.dynamic_slice_in_dim(kg, rs, kr, axis=1)[:, :, kcol]
        v_blk = lax.dynamic_slice_in_dim(vg, rs, kr, axis=1)[:, :, kcol]
        q_row = lax.dynamic_index_in_dim(qg, r, axis=1, keepdims=False)
        row_idx = rs + jnp.arange(kr) - r + NA_WIN_R - 1
        bias = rpb[:, row_idx[:, None, None, None], col_off[None]].astype(F32)
        bias = bias.transpose(0, 2, 3, 1, 4)
        s_nb = jnp.einsum('bjqhd,brjkhd->bhjqrk', q_row, k_blk, preferred_element_type=F32) * scale
        s_nb = jnp.where(col_mask_b, s_nb + bias, NEG_INF).reshape(B, H, n_cb, NA_QBLK, kr * NA_KBLK)
        s_cx = jnp.einsum('bjqhd,blhd->bhjql', q_row, kc, preferred_element_type=F32) * scale
        p = jax.nn.softmax(jnp.concatenate([s_nb, s_cx], axis=-1), axis=-1).astype(v.dtype)
        p_nb = p[..., :kr * NA_KBLK].reshape(B, H, n_cb, NA_QBLK, kr, NA_KBLK)
        p_cx = p[..., kr * NA_KBLK:]
        o = jnp.einsum('bhjqrk,brjkhd->bjqhd', p_nb, v_blk) + jnp.einsum('bhjql,blhd->bjqhd', p_cx, vc)
        return o.reshape(B, GRID_W, H, d)

    out = lax.map(row_block, jnp.arange(rows))
    return out.transpose(1, 0, 2, 3, 4).reshape(B, T, H * d)


def sliding_window_gqa(q, k, v, kc, vc, sinks):
    B, T, Hq, d = q.shape
    Hkv = k.shape[2]
    G = Hq // Hkv
    nb = T // SW_BLOCK
    scale = d ** -0.5
    qb = q.reshape(B, nb, SW_BLOCK, Hkv, G, d)

    def banded(t):
        tp = jnp.pad(t, ((0, 0), (SW_BLOCK, SW_BLOCK), (0, 0), (0, 0))).reshape(B, nb + 2, SW_BLOCK, Hkv, d)
        return jnp.concatenate([tp[:, :-2], tp[:, 1:-1], tp[:, 2:]], axis=2)

    kb, vb = banded(k), banded(v)
    rel = np.arange(3 * SW_BLOCK)[None, :] - SW_BLOCK - np.arange(SW_BLOCK)[:, None]
    kabs = np.arange(nb)[:, None, None] * SW_BLOCK + np.arange(3 * SW_BLOCK)[None, None, :] - SW_BLOCK
    valid = jnp.asarray((np.abs(rel)[None] <= SW_WINDOW) & (kabs >= 0) & (kabs < T))
    s_loc = jnp.einsum('bnqhgd,bnkhd->bhgnqk', qb, kb, preferred_element_type=F32) * scale
    s_loc = jnp.where(valid, s_loc, NEG_INF)
    s_cx = jnp.einsum('bnqhgd,blhd->bhgnql', qb, kc, preferred_element_type=F32) * scale
    sink = jnp.broadcast_to(sinks.astype(F32).reshape(Hkv, G)[None, :, :, None, None, None], s_loc.shape[:-1] + (1,))
    p = jax.nn.softmax(jnp.concatenate([s_loc, s_cx, sink], axis=-1), axis=-1)
    p_loc = p[..., :3 * SW_BLOCK].astype(v.dtype)
    p_cx = p[..., 3 * SW_BLOCK:-1].astype(v.dtype)
    o = jnp.einsum('bhgnqk,bnkhd->bnqhgd', p_loc, vb) + jnp.einsum('bhgnql,blhd->bnqhgd', p_cx, vc)
    return o.reshape(B, T, Hq * d)


def context_attention(q, k, v, sinks=None):
    B, L, Hq, d = q.shape
    Hkv = k.shape[2]
    G = Hq // Hkv
    qg = q.reshape(B, L, Hkv, G, d)
    s = jnp.einsum('blhgd,bmhd->bhglm', qg, k, preferred_element_type=F32) * d ** -0.5
    if sinks is not None:
        sink = jnp.broadcast_to(sinks.astype(F32).reshape(Hkv, G)[None, :, :, None, None], s.shape[:-1] + (1,))
        s = jnp.concatenate([s, sink], axis=-1)
    p = jax.nn.softmax(s, axis=-1)[..., :k.shape[1]].astype(v.dtype)
    o = jnp.einsum('bhglm,bmhd->blhgd', p, v)
    return o.reshape(B, L, Hq * d)


def depthwise_conv(u, w, b):
    kw = w.shape[0]
    left = (kw - 1) // 2
    out = lax.conv_general_dilated(u, w[:, None, :], window_strides=(1,), padding=[(left, kw - 1 - left)],
                                   dimension_numbers=('NWC', 'WIO', 'NWC'), feature_group_count=u.shape[-1])
    return out + b


def rglru_coeffs(u, w_a, b_a, w_x, b_x, lam):
    B, N, C = u.shape
    ub = u.reshape(B, N, LRU_BLOCKS, C // LRU_BLOCKS)
    r = jax.nn.sigmoid(jnp.einsum('bnki,kij->bnkj', ub, w_a, preferred_element_type=F32).reshape(B, N, C) + b_a.astype(F32))
    i = jax.nn.sigmoid(jnp.einsum('bnki,kij->bnkj', ub, w_x, preferred_element_type=F32).reshape(B, N, C) + b_x.astype(F32))
    log_a = -LRU_C * r * jax.nn.softplus(-lam.astype(F32))
    a = jnp.exp(log_a)
    b = jnp.sqrt(-jnp.expm1(2.0 * log_a)) * (i * u.astype(F32))
    return a, b


def linear_scan(a, b, h0, reverse):
    if h0 is not None:
        if reverse:
            b = b.at[:, -1].add(a[:, -1] * h0)
        else:
            b = b.at[:, 0].add(a[:, 0] * h0)

    def combine(e1, e2):
        a1, b1 = e1
        a2, b2 = e2
        return a1 * a2, a2 * b1 + b2

    _, h = lax.associative_scan(combine, (a, b), reverse=reverse, axis=1)
    return h


def merge_branches(ys, gate_logits, w_branch, w_out):
    y = jnp.stack(ys, axis=2)
    proj = jnp.einsum('bnkw,kwd->bnkd', y, w_branch)
    g = jax.nn.sigmoid(gate_logits.reshape(proj.shape).astype(F32)).astype(proj.dtype)
    return jnp.sum(g * proj, axis=2) @ w_out


def token_mixers(h, hc, w_in, na_rpb, conv_w, conv_b, lru_wa, lru_ba, lru_wx, lru_bx, lru_lam, sw_sinks,
                 w_branch, w_out, need_ctx_out):
    T = h.shape[1]
    qa, ka, va, gb, xb, qs, ks, vs, gl = split_cols(h @ w_in, PROJ_SPLITS)
    qac, kac, vac, gbc, xbc, qsc, ksc, vsc, glc = split_cols(hc @ w_in, PROJ_SPLITS)
    kac_h, vac_h = to_heads(kac, NA_HEADS), to_heads(vac, NA_HEADS)
    ksc_h, vsc_h = to_heads(ksc, SW_KV_HEADS), to_heads(vsc, SW_KV_HEADS)

    y_a = neighbourhood_attention(to_heads(qa, NA_HEADS), to_heads(ka, NA_HEADS), to_heads(va, NA_HEADS),
                                  kac_h, vac_h, na_rpb)

    u = depthwise_conv(xb, conv_w, conv_b)
    uc = depthwise_conv(xbc, conv_w, conv_b)
    h_lat, h_ctx = [], []
    for dr, reverse in enumerate((False, True)):
        ac, bc = rglru_coeffs(uc, lru_wa[dr], lru_ba[dr], lru_wx[dr], lru_bx[dr], lru_lam[dr])
        hcs = linear_scan(ac, bc, None, reverse)
        h_end = hcs[:, 0] if reverse else hcs[:, -1]
        al, bl = rglru_coeffs(u, lru_wa[dr], lru_ba[dr], lru_wx[dr], lru_bx[dr], lru_lam[dr])
        h_lat.append(linear_scan(al, bl, h_end, reverse))
        h_ctx.append(hcs)
    y_b = (jax.nn.gelu(gb.astype(F32)) * (h_lat[0] + h_lat[1])).astype(gb.dtype)

    cos, sin = axial_rope_tables(T)
    y_c = sliding_window_gqa(apply_axial_rope(to_heads(qs, SW_HEADS), cos, sin),
                             apply_axial_rope(to_heads(ks, SW_KV_HEADS), cos, sin),
                             to_heads(vs, SW_KV_HEADS), ksc_h, vsc_h, sw_sinks)

    out = merge_branches((y_a, y_b, y_c), gl, w_branch, w_out)
    if not need_ctx_out:
        return out, None
    y_ac = context_attention(to_heads(qac, NA_HEADS), kac_h, vac_h)
    y_bc = (jax.nn.gelu(gbc.astype(F32)) * (h_ctx[0] + h_ctx[1])).astype(gbc.dtype)
    y_cc = context_attention(to_heads(qsc, SW_HEADS), ksc_h, vsc_h, sw_sinks)
    out_c = merge_branches((y_ac, y_bc, y_cc), glc, w_branch, w_out)
    return out, out_c


def clamped_swiglu(hid):
    glu, lin = jnp.split(hid, 2, axis=-1)
    glu = jnp.minimum(glu, SWIGLU_LIMIT)
    lin = jnp.clip(lin, -SWIGLU_LIMIT, SWIGLU_LIMIT)
    return glu * jax.nn.sigmoid(SWIGLU_ALPHA * glu) * (lin + 1.0)


def moe_ffn(h, w_router, b_router, w1, b1, w2, b2):
    shp = h.shape
    t = h.reshape(-1, shp[-1])
    logits = jnp.dot(t, w_router, preferred_element_type=F32) + b_router.astype(F32)
    top_val, top_idx = lax.top_k(logits, TOP_K)
    top_w = jax.nn.softmax(top_val, axis=-1)
    gate = jnp.einsum('nk,nke->ne', top_w, jax.nn.one_hot(top_idx, N_EXPERTS, dtype=F32))
    out = jnp.zeros(t.shape, F32)
    for e in range(N_EXPERTS):
        y = clamped_swiglu(t @ w1[e] + b1[e]) @ w2[e] + b2[e]
        out = out + gate[:, e:e + 1] * y
    return out.reshape(shp).astype(h.dtype)


def layer(x, xc, c, c_ctx, w_ada, b_ada, g_mix_pre, g_mix_post, g_ffn_pre, g_ffn_post, w_in, na_rpb,
          conv_w, conv_b, lru_wa, lru_ba, lru_wx, lru_bx, lru_lam, sw_sinks, w_branch, w_out,
          w_router, b_router, w1, b1, w2, b2, need_ctx_out):
    mod = jax.nn.silu(c) @ w_ada + b_ada
    mod_c = jax.nn.silu(c_ctx) @ w_ada + b_ada
    sh1, sc1, g1, sh2, sc2, g2 = [m[:, None, :] for m in jnp.split(mod, 6, axis=-1)]
    sh1c, sc1c, g1c, sh2c, sc2c, g2c = jnp.split(mod_c, 6, axis=-1)

    h = modulate(rms_norm(x, g_mix_pre), sh1, sc1)
    hc = modulate(rms_norm(xc, g_mix_pre), sh1c, sc1c)
    y, yc = token_mixers(h, hc, w_in, na_rpb, conv_w, conv_b, lru_wa, lru_ba, lru_wx, lru_bx, lru_lam,
                         sw_sinks, w_branch, w_out, need_ctx_out)
    x = x + g1 * rms_norm(y, g_mix_post)
    h = modulate(rms_norm(x, g_ffn_pre), sh2, sc2)
    x = x + g2 * rms_norm(moe_ffn(h, w_router, b_router, w1, b1, w2, b2), g_ffn_post)
    if need_ctx_out:
        xc = xc + g1c * rms_norm(yc, g_mix_post)
        hc = modulate(rms_norm(xc, g_ffn_pre), sh2c, sc2c)
        xc = xc + g2c * rms_norm(moe_ffn(hc, w_router, b_router, w1, b1, w2, b2), g_ffn_post)
    return x, xc


def setup_inputs(seed: int = 0) -> dict:
    key = jax.random.key(seed)
    ks = jax.random.split(key, 32)

    def nrm(k, shape, s):
        return jax.random.normal(k, shape, F32) * s

    L = DEPTH
    bs = LRU_WIDTH // LRU_BLOCKS
    a_pow_c = jax.random.uniform(ks[18], (L, 2, LRU_WIDTH), F32, minval=0.9, maxval=0.999)
    a0 = a_pow_c ** (1.0 / LRU_C)
    return {
        'x': nrm(ks[0], (BATCH, SEQ, D_MODEL), 1.0),
        'c': nrm(ks[1], (BATCH, D_MODEL), 1.0),
        'ctx': nrm(ks[2], (BATCH, CTX_LEN, D_MODEL), 1.0),
        'c_ctx': nrm(ks[3], (D_MODEL,), 1.0),
        'w_ada': nrm(ks[4], (L, D_MODEL, 6 * D_MODEL), 0.5 * D_MODEL ** -0.5),
        'b_ada': nrm(ks[5], (L, 6 * D_MODEL), 0.02),
        'g_mix_pre': 1.0 + nrm(ks[6], (L, D_MODEL), 0.1),
        'g_mix_post': 1.0 + nrm(ks[7], (L, D_MODEL), 0.1),
        'g_ffn_pre': 1.0 + nrm(ks[8], (L, D_MODEL), 0.1),
        'g_ffn_post': 1.0 + nrm(ks[9], (L, D_MODEL), 0.1),
        'w_in': nrm(ks[10], (L, D_MODEL, PROJ_WIDTH), D_MODEL ** -0.5),
        'na_rpb': nrm(ks[11], (L, NA_HEADS, 2 * NA_WIN_R - 1, 2 * NA_WIN_C - 1), 0.5),
        'conv_w': nrm(ks[12], (L, LRU_CONV, LRU_WIDTH), LRU_CONV ** -0.5),
        'conv_b': nrm(ks[13], (L, LRU_WIDTH), 0.01),
        'lru_wa': nrm(ks[14], (L, 2, LRU_BLOCKS, bs, bs), bs ** -0.5),
        'lru_ba': nrm(ks[15], (L, 2, LRU_WIDTH), 0.1),
        'lru_wx': nrm(ks[16], (L, 2, LRU_BLOCKS, bs, bs), bs ** -0.5),
        'lru_bx': nrm(ks[17], (L, 2, LRU_WIDTH), 0.1),
        'lru_lam': jnp.log(a0) - jnp.log1p(-a0),
        'sw_sinks': nrm(ks[19], (L, SW_HEADS), 0.5),
        'w_branch': nrm(ks[20], (L, N_BRANCH, BRANCH_W, D_MODEL), BRANCH_W ** -0.5),
        'w_out': nrm(ks[21], (L, D_MODEL, D_MODEL), D_MODEL ** -0.5),
        'w_router': nrm(ks[22], (L, D_MODEL, N_EXPERTS), D_MODEL ** -0.5),
        'b_router': nrm(ks[23], (L, N_EXPERTS), 0.01),
        'w1': nrm(ks[24], (L, N_EXPERTS, D_MODEL, 2 * D_EXPERT), D_MODEL ** -0.5),
        'b1': nrm(ks[25], (L, N_EXPERTS, 2 * D_EXPERT), 0.01),
        'w2': nrm(ks[26], (L, N_EXPERTS, D_EXPERT, D_MODEL), D_EXPERT ** -0.5),
        'b2': nrm(ks[27], (L, N_EXPERTS, D_MODEL), 0.01),
    }


def reference(x, c, ctx, c_ctx, w_ada, b_ada, g_mix_pre, g_mix_post, g_ffn_pre, g_ffn_post, w_in, na_rpb,
              conv_w, conv_b, lru_wa, lru_ba, lru_wx, lru_bx, lru_lam, sw_sinks, w_branch, w_out,
              w_router, b_router, w1, b1, w2, b2):
    xc = ctx
    for l in range(DEPTH):
        x, xc = layer(x, xc, c, c_ctx, w_ada[l], b_ada[l], g_mix_pre[l], g_mix_post[l], g_ffn_pre[l],
                      g_ffn_post[l], w_in[l], na_rpb[l], conv_w[l], conv_b[l], lru_wa[l], lru_ba[l],
                      lru_wx[l], lru_bx[l], lru_lam[l], sw_sinks[l], w_branch[l], w_out[l],
                      w_router[l], b_router[l], w1[l], b1[l], w2[l], b2[l],
                      need_ctx_out=(l < DEPTH - 1))
    return x
```

```python
import functools

import numpy as np
import jax
import jax.numpy as jnp
from jax import lax
from jax.experimental import pallas as pl
from jax.experimental.pallas import tpu as pltpu

F32 = jnp.float32
BF16 = jnp.bfloat16
I32 = jnp.int32

D_MODEL = 1024
CTX_LEN = 256
GRID_W = 64
HEAD_DIM = 64
BRANCH_W = 512
N_BRANCH = 3
NA_HEADS = 8
NA_WIN_R = 8
NA_WIN_C = 16
LRU_WIDTH = 512
LRU_BLOCKS = 8
LRU_CONV = 4
LRU_C = 8.0
SW_HEADS = 8
SW_KV_HEADS = 2
SW_WINDOW = 128
SW_BLOCK = 128
ROPE_BASE = 10000.0
ROPE_AXIS_DIM = HEAD_DIM // 2
N_EXPERTS = 32
TOP_K = 4
D_EXPERT = 1024
SWIGLU_LIMIT = 7.0
SWIGLU_ALPHA = 1.702
RMS_EPS = 1e-6
NEG_INF = -1e30

LANES = 128
TILE = 256
NA_QROWS = 4
NA_KROWS = 12
MOE_ROWS = 256
DISPATCH_TILE = 512
VMEM_LIMIT = 56 * 1024 * 1024

_C_QA, _C_KA, _C_VA, _C_GB, _C_XB, _C_QS, _C_QSP, _C_KS, _C_KSP, _C_VS, _C_GL, _C_END = (
    0, 512, 1024, 1536, 2048, 2560, 3072, 3584, 3840, 4096, 4352, 7424)


def _cparams(sem, vmem=VMEM_LIMIT):
    return pltpu.CompilerParams(dimension_semantics=sem, vmem_limit_bytes=vmem)


def _dot(a, b):
    return jnp.dot(a, b, preferred_element_type=F32)


def _dot_t(a, b):
    return lax.dot_general(a, b, (((1,), (1,)), ((), ())), preferred_element_type=F32)


def _rms(x, g):
    return x * lax.rsqrt(jnp.mean(x * x, axis=-1, keepdims=True) + RMS_EPS) * g


def _ada_kernel(c_ref, w_ref, b_ref, o_ref):
    c = c_ref[...]
    s = c * jax.nn.sigmoid(c)
    o_ref[...] = jnp.dot(s, w_ref[...], preferred_element_type=F32, precision=lax.Precision.HIGHEST) + b_ref[...]


def _ada(c8, w_ada, b_ada):
    depth, d, n = w_ada.shape
    tn = 1536
    return pl.pallas_call(
        _ada_kernel,
        grid=(depth, n // tn),
        in_specs=[pl.BlockSpec((8, d), lambda l, j: (0, 0)),
                  pl.BlockSpec((None, d, tn), lambda l, j: (l, 0, j)),
                  pl.BlockSpec((None, 1, tn), lambda l, j: (l, 0, j))],
        out_specs=pl.BlockSpec((None, 8, tn), lambda l, j: (l, 0, j)),
        out_shape=jax.ShapeDtypeStruct((depth, 8, n), F32),
        compiler_params=_cparams(("arbitrary", "arbitrary")),
        name="ada_mod",
    )(c8, w_ada, b_ada.reshape(depth, 1, n))


def _mod_row(mod_ref, b, t, k):
    row = jnp.where(t == 0, 2, b)
    return mod_ref[pl.ds(row, 1), k * D_MODEL:(k + 1) * D_MODEL]


def _inproj_kernel(x_ref, mod_ref, g_ref, w_ref, cos_ref, sin_ref,
                   qa_ref, ka_ref, va_ref, gb_ref, xb_ref, qs_ref, ks_ref, vs_ref, sg_ref):
    b, t = pl.program_id(0), pl.program_id(1)
    x = x_ref[...]
    h = _rms(x, g_ref[...]) * (1.0 + _mod_row(mod_ref, b, t, 1)) + _mod_row(mod_ref, b, t, 0)
    h = h.astype(BF16)

    def seg(lo, hi):
        return _dot(h, w_ref[:, lo:hi])

    qa_ref[...] = seg(_C_QA, _C_KA).astype(BF16)
    ka_ref[...] = seg(_C_KA, _C_VA).astype(BF16)
    va_ref[...] = seg(_C_VA, _C_GB).astype(BF16)
    gb_ref[...] = seg(_C_GB, _C_XB)
    xb_ref[...] = seg(_C_XB, _C_QS)
    cos = cos_ref[...]
    sin = sin_ref[...]
    for j in range(4):
        q = seg(_C_QS + j * LANES, _C_QS + (j + 1) * LANES)
        qp = seg(_C_QSP + j * LANES, _C_QSP + (j + 1) * LANES)
        qs_ref[:, j * LANES:(j + 1) * LANES] = (q * cos + qp * sin).astype(BF16)
    for j in range(2):
        k = seg(_C_KS + j * LANES, _C_KS + (j + 1) * LANES)
        kp = seg(_C_KSP + j * LANES, _C_KSP + (j + 1) * LANES)
        ks_ref[:, j * LANES:(j + 1) * LANES] = (k * cos + kp * sin).astype(BF16)
    vs_ref[...] = seg(_C_VS, _C_GL).astype(BF16)
    for j in range(N_BRANCH):
        lo = _C_GL + j * D_MODEL
        sg_ref[:, j * D_MODEL:(j + 1) * D_MODEL] = jax.nn.sigmoid(seg(lo, lo + D_MODEL))


def _inproj(x_all, mod, g_pre, w_cat, cos_t, sin_t):
    bsz, s, d = x_all.shape
    nt = s // TILE
    tok = lambda b, t: (b, t, 0)
    const2 = lambda b, t: (0, 0)

    def out(width, dtype):
        return jax.ShapeDtypeStruct((bsz, s, width), dtype), pl.BlockSpec((None, TILE, width), tok)

    outs = [out(512, BF16), out(512, BF16), out(512, BF16), out(512, F32), out(512, F32),
            out(512, BF16), out(256, BF16), out(256, BF16), out(N_BRANCH * D_MODEL, F32)]
    return pl.pallas_call(
        _inproj_kernel,
        grid=(bsz, nt),
        in_specs=[pl.BlockSpec((None, TILE, d), tok),
                  pl.BlockSpec((8, 6 * d), const2),
                  pl.BlockSpec((1, d), const2),
                  pl.BlockSpec((d, _C_END), const2),
                  pl.BlockSpec((TILE, LANES), lambda b, t: (t, 0)),
                  pl.BlockSpec((TILE, LANES), lambda b, t: (t, 0))],
        out_specs=[o[1] for o in outs],
        out_shape=[o[0] for o in outs],
        compiler_params=_cparams(("arbitrary", "arbitrary")),
        name="inproj",
    )(x_all, mod, g_pre, w_cat, cos_t, sin_t)


def _two_head_attention(q, score_fn, value_fn):
    lane = lax.broadcasted_iota(I32, (1, LANES), 1)
    low = lane < HEAD_DIM
    outs = []
    for hh in range(2):
        qm = jnp.where(low if hh == 0 else jnp.logical_not(low), q, jnp.zeros_like(q))
        blocks = score_fn(qm, hh)
        m = blocks[0].max(axis=-1, keepdims=True)
        for s in blocks[1:]:
            m = jnp.maximum(m, s.max(axis=-1, keepdims=True))
        ps = [jnp.exp(s - m) for s in blocks]
        l = ps[0].sum(axis=-1, keepdims=True)
        for p in ps[1:]:
            l = l + p.sum(axis=-1, keepdims=True)
        o = value_fn([p.astype(BF16) for p in ps])
        outs.append(o / l)
    return jnp.where(low, outs[0], outs[1])


def _na_kernel(q_ref, k_ref, v_ref, tab_ref, o_ref, *, t_off, t_lat):
    t = pl.program_id(2) + t_off
    q = q_ref[...]
    kc = k_ref[0:CTX_LEN, :]
    vc = v_ref[0:CTX_LEN, :]

    if t_off == 0:
        @pl.when(t == 0)
        def _ctx():
            o = _two_head_attention(q, lambda qm, hh: [_dot_t(qm, kc)], lambda ps: _dot(ps[0], vc))
            o_ref[...] = o.astype(BF16)

    @pl.when(t > 0)
    def _lat():
        nk = NA_KROWS * GRID_W
        start = jnp.clip((t - 2) * TILE, 0, t_lat - nk)
        start = pl.multiple_of(start + CTX_LEN, TILE)
        k = k_ref[pl.ds(start, nk), :]
        v = v_ref[pl.ds(start, nk), :]
        o = _two_head_attention(
            q,
            lambda qm, hh: [_dot_t(qm, k) + tab_ref[hh], _dot_t(qm, kc)],
            lambda ps: _dot(ps[0], v) + _dot(ps[1], vc))
        o_ref[...] = o.astype(BF16)


def _na_attention(qa, ka, va, table, with_ctx):
    bsz, s, _ = qa.shape
    t_lat = s - CTX_LEN
    nrb = t_lat // TILE
    t_off = 0 if with_ctx else 1
    steps = nrb + 1 - t_off
    out_rows = steps * TILE

    def var(i):
        rb = jnp.maximum(i + t_off - 1, 0)
        return jnp.where(rb == 0, 0, jnp.where(rb == nrb - 1, 2, 1))

    return pl.pallas_call(
        functools.partial(_na_kernel, t_off=t_off, t_lat=t_lat),
        grid=(bsz, NA_HEADS // 2, steps),
        in_specs=[pl.BlockSpec((None, TILE, LANES), lambda b, p, i: (b, i + t_off, p)),
                  pl.BlockSpec((None, s, LANES), lambda b, p, i: (b, 0, p)),
                  pl.BlockSpec((None, s, LANES), lambda b, p, i: (b, 0, p)),
                  pl.BlockSpec((None, 2, TILE, NA_KROWS * GRID_W), lambda b, p, i: (var(i), p, 0, 0))],
        out_specs=pl.BlockSpec((None, TILE, LANES), lambda b, p, i: (b, i, p)),
        out_shape=jax.ShapeDtypeStruct((bsz, out_rows, BRANCH_W), BF16),
        compiler_params=_cparams(("arbitrary", "arbitrary", "arbitrary")),
        name="na_attention",
    )(qa, ka, va, table)


def _na_bias_table(rpb, rows):
    qr = np.arange(NA_QROWS)
    kr = np.arange(NA_KROWS)
    r0 = np.array([0, NA_WIN_R // 2, rows - NA_QROWS])
    kr0 = np.array([0, 0, rows - NA_KROWS])
    q_abs = r0[:, None] + qr[None, :]
    k_abs = kr0[:, None] + kr[None, :]
    rs = np.clip(q_abs - NA_WIN_R // 2, 0, rows - NA_WIN_R)
    rvalid = (k_abs[:, None, :] >= rs[:, :, None]) & (k_abs[:, None, :] < rs[:, :, None] + NA_WIN_R)
    ridx = np.clip(k_abs[:, None, :] - q_abs[:, :, None] + NA_WIN_R - 1, 0, 2 * NA_WIN_R - 2)
    qc = np.arange(GRID_W)
    kc = np.arange(GRID_W)
    q_start = np.clip(qc - NA_WIN_C // 2, 0, GRID_W - NA_WIN_C)
    cvalid = (kc[None, :] >= q_start[:, None]) & (kc[None, :] < q_start[:, None] + NA_WIN_C)
    cidx = np.clip(kc[None, :] - qc[:, None] + NA_WIN_C - 1, 0, 2 * NA_WIN_C - 2)
    valid = rvalid[:, :, None, :, None] & cvalid[None, None, :, None, :]
    tab = rpb[:, ridx[:, :, None, :, None], cidx[None, None, :, None, :]].astype(F32)
    tab = jnp.where(jnp.asarray(valid)[None], tab, NEG_INF)
    tab = tab.transpose(1, 0, 2, 3, 4, 5)
    return tab.reshape(3, NA_HEADS, NA_QROWS * GRID_W, NA_KROWS * GRID_W)


def _sw_kernel(q_ref, k_ref, v_ref, mask_ref, sink_ref, o_ref, *, t_off, t_lat):
    t = pl.program_id(2) + t_off
    q = q_ref[...]
    lane = lax.broadcasted_iota(I32, (1, LANES), 1)
    low = lane < HEAD_DIM
    zero = jnp.zeros((SW_BLOCK, LANES), BF16)
    q01, q23 = q[:, :LANES], q[:, LANES:]
    qs = jnp.concatenate([jnp.where(low, q01, zero), jnp.where(low, zero, q01),
                          jnp.where(low, q23, zero), jnp.where(low, zero, q23)], axis=0)
    kc = k_ref[0:CTX_LEN, :]
    vc = v_ref[0:CTX_LEN, :]
    sink = sink_ref[...]
    s_cx = _dot_t(qs, kc)

    def finish(blocks, vals):
        m = sink
        for s in blocks:
            m = jnp.maximum(m, s.max(axis=-1, keepdims=True))
        ps = [jnp.exp(s - m) for s in blocks]
        l = jnp.exp(sink - m)
        for p in ps:
            l = l + p.sum(axis=-1, keepdims=True)
        o = _dot(ps[0].astype(BF16), vals[0])
        for p, v in zip(ps[1:], vals[1:]):
            o = o + _dot(p.astype(BF16), v)
        o = o / l
        b = SW_BLOCK
        o_ref[:, :LANES] = jnp.where(low, o[0:b], o[b:2 * b]).astype(BF16)
        o_ref[:, LANES:] = jnp.where(low, o[2 * b:3 * b], o[3 * b:4 * b]).astype(BF16)

    n_ctx_tiles = CTX_LEN // SW_BLOCK
    if t_off == 0:
        @pl.when(t < n_ctx_tiles)
        def _ctx():
            finish([s_cx], [vc])

    @pl.when(t >= n_ctx_tiles)
    def _lat():
        nk = 3 * SW_BLOCK
        start = jnp.clip((t - n_ctx_tiles - 1) * SW_BLOCK, 0, t_lat - nk)
        start = pl.multiple_of(start + CTX_LEN, SW_BLOCK)
        k = k_ref[pl.ds(start, nk), :]
        v = v_ref[pl.ds(start, nk), :]
        finish([_dot_t(qs, k) + mask_ref[...], s_cx], [v, vc])


def _sw_attention(qs, ks, vs, mask, sink_col, with_ctx):
    bsz, s, _ = qs.shape
    t_lat = s - CTX_LEN
    nb = t_lat // SW_BLOCK
    n_ctx_tiles = CTX_LEN // SW_BLOCK
    t_off = 0 if with_ctx else n_ctx_tiles
    steps = nb + n_ctx_tiles - t_off

    def var(i):
        n = jnp.maximum(i + t_off - n_ctx_tiles, 0)
        return jnp.where(n == 0, 0, jnp.where(n == nb - 1, 2, 1))

    g = SW_HEADS // SW_KV_HEADS
    return pl.pallas_call(
        functools.partial(_sw_kernel, t_off=t_off, t_lat=t_lat),
        grid=(bsz, SW_KV_HEADS, steps),
        in_specs=[pl.BlockSpec((None, SW_BLOCK, 2 * LANES), lambda b, h, i: (b, i + t_off, h)),
                  pl.BlockSpec((None, s, LANES), lambda b, h, i: (b, 0, h)),
                  pl.BlockSpec((None, s, LANES), lambda b, h, i: (b, 0, h)),
                  pl.BlockSpec((None, g * SW_BLOCK, 3 * SW_BLOCK), lambda b, h, i: (var(i), 0, 0)),
                  pl.BlockSpec((None, g * SW_BLOCK, 1), lambda b, h, i: (h, 0, 0))],
        out_specs=pl.BlockSpec((None, SW_BLOCK, 2 * LANES), lambda b, h, i: (b, i, h)),
        out_shape=jax.ShapeDtypeStruct((bsz, steps * SW_BLOCK, BRANCH_W), BF16),
        compiler_params=_cparams(("arbitrary", "arbitrary", "arbitrary")),
        name="sw_attention",
    )(qs, ks, vs, mask, sink_col)


def _sw_mask_table():
    i = np.arange(SW_BLOCK)[:, None]
    j = np.arange(3 * SW_BLOCK)[None, :]
    tabs = []
    for shift in (0, SW_BLOCK, 2 * SW_BLOCK):
        rel = j - i - shift
        tabs.append(np.where(np.abs(rel) <= SW_WINDOW, 0.0, NEG_INF).astype(np.float32))
    tab = np.stack(tabs)
    return jnp.asarray(np.tile(tab, (1, SW_HEADS // SW_KV_HEADS, 1)))


def _lru_kernel(x_ref, xp_ref, xn_ref, cw_ref, cb_ref, wg_ref, bg_ref, c8_ref, o_ref,
                xs, a_s, b_s, carry, *, nc):
    d, i = pl.program_id(1), pl.program_id(2)
    ci = jnp.where(d == 0, i, jnp.where(i == 0, 0, nc - i))
    tc = TILE
    seg = tc // 8
    w = LRU_WIDTH

    no_prev = (ci == 0) | (ci == 1)
    no_next = (ci == 0) | (ci == nc - 1)
    xs[0:8, :] = jnp.where(no_prev, 0.0, xp_ref[...])
    xs[8:8 + tc, :] = x_ref[...]
    xs[8 + tc:16 + tc, :] = jnp.where(no_next, 0.0, xn_ref[...])
    u = cb_ref[...] + cw_ref[0:1, :] * xs[7:7 + tc, :]
    for j in range(1, LRU_CONV):
        u = u + cw_ref[j:j + 1, :] * xs[7 + j:7 + j + tc, :]

    g = _dot(u.astype(BF16), wg_ref[...]) + bg_ref[...]
    r = jax.nn.sigmoid(g[:, :w])
    ig = jax.nn.sigmoid(g[:, w:])
    log_a = c8_ref[...] * r
    a = jnp.exp(log_a)
    th = jnp.tanh(log_a)
    bb = jnp.sqrt(-2.0 * th / (1.0 - th)) * (ig * u)
    nl = w // LANES
    for cidx in range(nl):
        a_s[cidx] = a[:, cidx * LANES:(cidx + 1) * LANES]
        b_s[cidx] = bb[:, cidx * LANES:(cidx + 1) * LANES]

    @pl.when(i == 0)
    def _init():
        carry[...] = jnp.zeros_like(carry)

    def body(jj, hp):
        hs, ps = hp
        j = jnp.where(d == 0, jj, seg - 1 - jj)
        hs_new, ps_new = [], []
        for cidx in range(nl):
            at = a_s[cidx, pl.ds(j, 8, stride=seg), :]
            bt = b_s[cidx, pl.ds(j, 8, stride=seg), :]
            h = at * hs[cidx] + bt
            p = at * ps[cidx]
            b_s[cidx, pl.ds(j, 8, stride=seg), :] = h
            a_s[cidx, pl.ds(j, 8, stride=seg), :] = p
            hs_new.append(h)
            ps_new.append(p)
        return tuple(hs_new), tuple(ps_new)

    init = (tuple(jnp.zeros((8, LANES), F32) for _ in range(nl)), tuple(jnp.ones((8, LANES), F32) for _ in range(nl)))
    hs_end, ps_end = lax.fori_loop(0, seg, body, init)
    h_end = jnp.concatenate(hs_end, axis=1)
    p_end = jnp.concatenate(ps_end, axis=1)

    c_in = carry[0:1, :]

    def fix(sidx, c):
        rows = slice(sidx * seg, (sidx + 1) * seg)
        for cidx in range(nl):
            lanes = slice(cidx * LANES, (cidx + 1) * LANES)
            o_ref[rows, lanes] = b_s[cidx, rows, :] + a_s[cidx, rows, :] * c[:, lanes]
        return h_end[sidx:sidx + 1, :] + p_end[sidx:sidx + 1, :] * c

    @pl.when(d == 0)
    def _fwd():
        c = c_in
        for sidx in range(8):
            c = fix(sidx, c)
        carry[0:1, :] = c

    @pl.when(d == 1)
    def _bwd():
        c = c_in
        for sidx in range(7, -1, -1):
            c = fix(sidx, c)
        carry[0:1, :] = c


def _lru(xb, conv_w, conv_b, wg, bg, c8):
    bsz, s, w = xb.shape
    nc = s // TILE
    r8 = TILE // 8

    def chunk(d, i):
        return jnp.where(d == 0, i, jnp.where(i == 0, 0, nc - i))

    return pl.pallas_call(
        functools.partial(_lru_kernel, nc=nc),
        grid=(bsz, 2, nc),
        in_specs=[pl.BlockSpec((None, TILE, w), lambda b, d, i: (b, chunk(d, i), 0)),
                  pl.BlockSpec((None, 8, w), lambda b, d, i: (b, jnp.maximum(chunk(d, i) * r8 - 1, 0), 0)),
                  pl.BlockSpec((None, 8, w), lambda b, d, i: (b, jnp.minimum((chunk(d, i) + 1) * r8, s // 8 - 1), 0)),
                  pl.BlockSpec((LRU_CONV, w), lambda b, d, i: (0, 0)),
                  pl.BlockSpec((1, w), lambda b, d, i: (0, 0)),
                  pl.BlockSpec((None, w, 2 * w), lambda b, d, i: (d, 0, 0)),
                  pl.BlockSpec((None, 1, 2 * w), lambda b, d, i: (d, 0, 0)),
                  pl.BlockSpec((None, 1, w), lambda b, d, i: (d, 0, 0))],
        out_specs=pl.BlockSpec((None, None, TILE, w), lambda b, d, i: (d, b, chunk(d, i), 0)),
        out_shape=jax.ShapeDtypeStruct((2, bsz, s, w), F32),
        scratch_shapes=[pltpu.VMEM((TILE + 16, w), F32), pltpu.VMEM((w // LANES, TILE, LANES), F32),
                        pltpu.VMEM((w // LANES, TILE, LANES), F32), pltpu.VMEM((8, w), F32)],
        compiler_params=_cparams(("arbitrary", "arbitrary", "arbitrary")),
        name="rglru",
    )(xb, xb, xb, conv_w, conv_b, wg, bg, c8)


def _gelu_tanh(x):
    return 0.5 * x * (1.0 + jnp.tanh(np.sqrt(2.0 / np.pi).astype(np.float32) * (x + 0.044715 * (x * x * x))))


def _merge_kernel(x_ref, ya_ref, hf_ref, hb_ref, gb_ref, yc_ref, sg_ref, mod_ref, gpost_ref, gpre_ref,
                  wbr_ref, wout_ref, wr_ref, br_ref, cnt0_ref,
                  x1_ref, h2_ref, idx_ref, wgt_ref, rank_ref, cnt_ref, cnt_s, *, t_off):
    b, ti = pl.program_id(0), pl.program_id(1)
    t = ti + t_off
    d = D_MODEL

    @pl.when((b == 0) & (ti == 0))
    def _init():
        cnt_s[...] = cnt0_ref[...]

    yb = (_gelu_tanh(gb_ref[...]) * (hf_ref[...] + hb_ref[...])).astype(BF16)
    ys = (ya_ref[...], yb, yc_ref[...])
    mix = sg_ref[:, 0:d] * _dot(ys[0], wbr_ref[0])
    for k in range(1, N_BRANCH):
        mix = mix + sg_ref[:, k * d:(k + 1) * d] * _dot(ys[k], wbr_ref[k])
    y = _dot(mix.astype(BF16), wout_ref[...])
    x1 = x_ref[...] + _mod_row(mod_ref, b, t, 2) * _rms(y, gpost_ref[...])
    x1_ref[...] = x1
    h2 = _rms(x1, gpre_ref[...]) * (1.0 + _mod_row(mod_ref, b, t, 4)) + _mod_row(mod_ref, b, t, 3)
    h2_ref[...] = h2

    logits = _dot(h2.astype(BF16), wr_ref[...]) + br_ref[...]
    lane = lax.broadcasted_iota(I32, (TILE, LANES), 1)
    work = logits
    tops, sels = [], []
    for k in range(TOP_K):
        m = work.max(axis=-1, keepdims=True)
        first = jnp.where(work == m, lane, LANES).min(axis=-1, keepdims=True)
        sel = lane == first
        work = jnp.where(sel, -3.0e38, work)
        tops.append(m)
        sels.append(sel)
    es = [jnp.exp(m - tops[0]) for m in tops]
    den = es[0] + es[1] + es[2] + es[3]

    chosen = jnp.zeros((TILE, LANES), F32)
    for sel in sels:
        chosen = jnp.where(sel, 1.0, chosen)
    rr = lax.broadcasted_iota(I32, (TILE, TILE), 0)
    cc = lax.broadcasted_iota(I32, (TILE, TILE), 1)
    tri = jnp.where(cc < rr, 1.0, 0.0).astype(BF16)
    pref = _dot(tri, chosen.astype(BF16)) + cnt_s[0:1, :]
    cnt_new = cnt_s[0:1, :] + chosen.sum(axis=0, keepdims=True)
    cnt_s[0:1, :] = cnt_new
    cnt_ref[...] = jnp.broadcast_to(cnt_new, cnt_ref.shape)

    idx_o = jnp.zeros((TILE, LANES), I32)
    wgt_o = jnp.zeros((TILE, LANES), F32)
    rank_o = jnp.zeros((TILE, LANES), I32)
    for k in range(TOP_K):
        first = jnp.where(sels[k], lane, 0).max(axis=-1, keepdims=True)
        rk = jnp.where(sels[k], pref, 0.0).sum(axis=-1, keepdims=True)
        idx_o = jnp.where(lane == k, first, idx_o)
        wgt_o = jnp.where(lane == k, es[k] / den, wgt_o)
        rank_o = jnp.where(lane == k, rk.astype(I32), rank_o)
    idx_ref[...] = idx_o
    wgt_ref[...] = wgt_o
    rank_ref[...] = rank_o


def _merge(x_all, ya, hfb, gb, yc, sg, mod, g_post, g_pre, w_br, w_out, w_r, b_r, cnt0, with_ctx):
    bsz, s, d = x_all.shape
    t_off = 0 if with_ctx else 1
    nt = s // TILE - t_off
    rows = nt * TILE
    inp = lambda b, t: (b, t + t_off, 0)
    loc = lambda b, t: (b, t, 0)
    c2 = lambda b, t: (0, 0)
    c3 = lambda b, t: (0, 0, 0)

    def out(width, dtype):
        return jax.ShapeDtypeStruct((bsz, rows, width), dtype), pl.BlockSpec((None, TILE, width), loc)

    outs = [out(d, F32), out(d, F32), out(LANES, I32), out(LANES, F32), out(LANES, I32)]
    shapes = [o[0] for o in outs] + [jax.ShapeDtypeStruct((8, LANES), F32)]
    specs = [o[1] for o in outs] + [pl.BlockSpec((8, LANES), c2)]
    return pl.pallas_call(
        functools.partial(_merge_kernel, t_off=t_off),
        grid=(bsz, nt),
        in_specs=[pl.BlockSpec((None, TILE, d), inp),
                  pl.BlockSpec((None, TILE, BRANCH_W), loc),
                  pl.BlockSpec((None, None, TILE, BRANCH_W), lambda b, t: (0, b, t + t_off, 0)),
                  pl.BlockSpec((None, None, TILE, BRANCH_W), lambda b, t: (1, b, t + t_off, 0)),
                  pl.BlockSpec((None, TILE, BRANCH_W), inp),
                  pl.BlockSpec((None, TILE, BRANCH_W), loc),
                  pl.BlockSpec((None, TILE, N_BRANCH * d), inp),
                  pl.BlockSpec((8, 6 * d), c2),
                  pl.BlockSpec((1, d), c2),
                  pl.BlockSpec((1, d), c2),
                  pl.BlockSpec((N_BRANCH, BRANCH_W, d), c3),
                  pl.BlockSpec((d, d), c2),
                  pl.BlockSpec((d, LANES), c2),
                  pl.BlockSpec((1, LANES), c2),
                  pl.BlockSpec((8, LANES), c2)],
        out_specs=specs,
        out_shape=shapes,
        scratch_shapes=[pltpu.VMEM((8, LANES), F32)],
        compiler_params=_cparams(("arbitrary", "arbitrary")),
        name="merge_router",
    )(x_all, ya, hfb, hfb, gb, yc, sg, mod, g_post, g_pre, w_br, w_out, w_r, b_r, cnt0)


def _dispatch_kernel(pos_ref, h_ref, xs_in_ref, xs_ref, sem):
    del xs_in_ref

    def row_copy(i, k):
        p = pos_ref[0, 0, i * TOP_K + k]
        return pltpu.make_async_copy(h_ref.at[pl.ds(i, 1)], xs_ref.at[pl.ds(p, 1)], sem)

    def start(i, c):
        for k in range(TOP_K):
            row_copy(i, k).start()
        return c

    lax.fori_loop(0, TILE, start, 0)

    def wait(i, c):
        for k in range(TOP_K):
            row_copy(i, k).wait()
        return c

    lax.fori_loop(0, TILE, wait, 0)


def _dispatch(h2, pos, n_rows):
    bsz, rows, d = h2.shape
    nt = rows // TILE
    zeros = jnp.zeros((n_rows, d), F32)
    return pl.pallas_call(
        _dispatch_kernel,
        grid=(bsz, nt),
        in_specs=[pl.BlockSpec((1, 1, TILE * TOP_K), lambda b, t: (b * nt + t, 0, 0), memory_space=pltpu.SMEM),
                  pl.BlockSpec((TILE, d), lambda b, t: (b * nt + t, 0)),
                  pl.BlockSpec(memory_space=pl.ANY)],
        out_specs=pl.BlockSpec(memory_space=pl.ANY),
        out_shape=jax.ShapeDtypeStruct((n_rows, d), F32),
        scratch_shapes=[pltpu.SemaphoreType.DMA(())],
        input_output_aliases={2: 0},
        compiler_params=_cparams(("arbitrary", "arbitrary")),
        name="moe_dispatch",
    )(pos.reshape(bsz * nt, 1, TILE * TOP_K), h2.reshape(bsz * rows, d), zeros)


def _expert_kernel(te_ref, nu_ref, x_ref, w1_ref, b1_ref, w2_ref, b2_ref, y_ref, w1_s, w2_s):
    t = pl.program_id(0)
    prev = te_ref[jnp.maximum(t - 1, 0)]
    active = t < nu_ref[0]

    @pl.when(active & ((t == 0) | (te_ref[t] != prev)))
    def _load():
        w1_s[...] = w1_ref[...].astype(BF16)
        w2_s[...] = w2_ref[...].astype(BF16)

    @pl.when(active)
    def _run():
        f = D_EXPERT
        x = x_ref[...].astype(BF16)
        hid = _dot(x, w1_s[...]) + b1_ref[...]
        glu = jnp.minimum(hid[:, :f], SWIGLU_LIMIT)
        lin = jnp.clip(hid[:, f:], -SWIGLU_LIMIT, SWIGLU_LIMIT)
        act = glu * jax.nn.sigmoid(SWIGLU_ALPHA * glu) * (lin + 1.0)
        y_ref[...] = _dot(act.astype(BF16), w2_s[...]) + b2_ref[...]

    @pl.when(jnp.logical_not(active))
    def _idle():
        y_ref[...] = jnp.zeros_like(y_ref)


def _experts(xs, tile_expert, n_used, w1, b1, w2, b2, layer):
    n_rows, d = xs.shape
    n_tiles = n_rows // MOE_ROWS
    depth, e, _, f2 = w1.shape
    xmap = lambda t, te, nu: (jnp.minimum(t, nu[0] - 1), 0)
    wmap = lambda t, te, nu: (layer, te[t], 0, 0)
    grid_spec = pltpu.PrefetchScalarGridSpec(
        num_scalar_prefetch=2,
        grid=(n_tiles,),
        in_specs=[pl.BlockSpec((MOE_ROWS, d), xmap),
                  pl.BlockSpec((None, None, d, f2), wmap),
                  pl.BlockSpec((None, None, 1, f2), wmap),
                  pl.BlockSpec((None, None, f2 // 2, d), wmap),
                  pl.BlockSpec((None, None, 1, d), wmap)],
        out_specs=pl.BlockSpec((MOE_ROWS, d), lambda t, te, nu: (t, 0)),
        scratch_shapes=[pltpu.VMEM((d, f2), BF16), pltpu.VMEM((f2 // 2, d), BF16)])
    return pl.pallas_call(
        _expert_kernel,
        grid_spec=grid_spec,
        out_shape=jax.ShapeDtypeStruct((n_rows, d), F32),
        compiler_params=_cparams(("arbitrary",)),
        name="moe_experts",
    )(tile_expert, n_used, xs, w1, b1.reshape(depth, e, 1, f2), w2, b2.reshape(depth, e, 1, d))


def _combine_kernel(pos_ref, ys_ref, wgt_ref, x1_ref, mod_ref, gpost_ref, o_ref, buf, sem, *, t_off):
    b, t = pl.program_id(0), pl.program_id(1) + t_off

    def row_copy(i, k):
        p = pos_ref[0, 0, i * TOP_K + k]
        return pltpu.make_async_copy(ys_ref.at[pl.ds(p, 1)], buf.at[k, pl.ds(i, 1)], sem)

    def start(i, c):
        for k in range(TOP_K):
            row_copy(i, k).start()
        return c

    lax.fori_loop(0, TILE, start, 0)

    def wait(i, c):
        for k in range(TOP_K):
            row_copy(i, k).wait()
        return c

    lax.fori_loop(0, TILE, wait, 0)

    wgt = wgt_ref[...]
    moe = wgt[:, 0:1] * buf[0]
    for k in range(1, TOP_K):
        moe = moe + wgt[:, k:k + 1] * buf[k]
    o_ref[...] = x1_ref[...] + _mod_row(mod_ref, b, t, 5) * _rms(moe, gpost_ref[...])


def _combine(ys, pos, wgt, x1, mod, g_post, with_ctx):
    bsz, rows, d = x1.shape
    nt = rows // TILE
    tok = lambda b, t: (b, t, 0)
    return pl.pallas_call(
        functools.partial(_combine_kernel, t_off=0 if with_ctx else 1),
        grid=(bsz, nt),
        in_specs=[pl.BlockSpec((1, 1, TILE * TOP_K), lambda b, t: (b * nt + t, 0, 0), memory_space=pltpu.SMEM),
                  pl.BlockSpec(memory_space=pl.ANY),
                  pl.BlockSpec((None, TILE, LANES), tok),
                  pl.BlockSpec((None, TILE, d), tok),
                  pl.BlockSpec((8, 6 * d), lambda b, t: (0, 0)),
                  pl.BlockSpec((1, d), lambda b, t: (0, 0))],
        out_specs=pl.BlockSpec((None, TILE, d), tok),
        out_shape=jax.ShapeDtypeStruct((bsz, rows, d), F32),
        scratch_shapes=[pltpu.VMEM((TOP_K, TILE, d), F32), pltpu.SemaphoreType.DMA(())],
        compiler_params=_cparams(("arbitrary", "arbitrary")),
        name="moe_combine",
    )(pos.reshape(bsz * nt, 1, TILE * TOP_K), ys, wgt, x1, mod, g_post)


def _moe(h2, idx, wgt, rank, counts, x1, mod, g_post, w1, b1, w2, b2, layer, with_ctx):
    bsz, rows, d = h2.shape
    n_tiles = (bsz * rows * TOP_K) // MOE_ROWS + N_EXPERTS
    n_rows = n_tiles * MOE_ROWS
    cnt = counts.astype(I32)
    tiles_e = (cnt + MOE_ROWS - 1) // MOE_ROWS
    tile_end = jnp.cumsum(tiles_e)
    row_start = (tile_end - tiles_e) * MOE_ROWS
    n_used = tile_end[-1:]
    tile_ids = jnp.minimum(jnp.arange(n_tiles, dtype=I32), n_used[0] - 1)
    tile_expert = jnp.minimum(jnp.searchsorted(tile_end, tile_ids, side="right"), N_EXPERTS - 1).astype(I32)
    pos = (row_start[idx[..., :TOP_K]] + rank[..., :TOP_K]).astype(I32)
    xs = _dispatch(h2, pos, n_rows)
    ys = _experts(xs, tile_expert, n_used.astype(I32), w1, b1, w2, b2, layer)
    return _combine(ys, pos, wgt, x1, mod, g_post, with_ctx)


def _rope_partner_perm(n_heads):
    d = np.arange(HEAD_DIM)
    half = ROPE_AXIS_DIM // 2
    p = np.where((d % ROPE_AXIS_DIM) < half, d + half, d - half)
    return (np.arange(n_heads)[:, None] * HEAD_DIM + p[None, :]).reshape(-1)


def _inproj_weight(w_in):
    sizes = (BRANCH_W, BRANCH_W, BRANCH_W, LRU_WIDTH, LRU_WIDTH, BRANCH_W,
             SW_KV_HEADS * HEAD_DIM, SW_KV_HEADS * HEAD_DIM, N_BRANCH * D_MODEL)
    offs = np.cumsum((0,) + sizes)
    qa, ka, va, gb, xb, qs, ks, vs, gl = [w_in[:, offs[i]:offs[i + 1]] for i in range(9)]
    scale = HEAD_DIM ** -0.5
    dup = np.repeat(np.arange(SW_KV_HEADS), 2)[:, None] * HEAD_DIM + np.arange(HEAD_DIM)[None, :]
    dup = dup.reshape(-1)
    qs = qs * scale
    ksd = ks[:, dup]
    cols = [qa * scale, ka, va, gb, xb, qs, qs[:, _rope_partner_perm(SW_HEADS)],
            ksd, ksd[:, _rope_partner_perm(2 * SW_KV_HEADS)], vs[:, dup], gl]
    return jnp.concatenate(cols, axis=1).astype(BF16)


def _rope_tables(t_lat):
    pos = np.arange(t_lat)
    row = (pos // GRID_W).astype(np.float32)
    col = (pos % GRID_W).astype(np.float32)
    half = ROPE_AXIS_DIM // 2
    inv = (ROPE_BASE ** (-jnp.arange(0, ROPE_AXIS_DIM, 2, dtype=F32) / ROPE_AXIS_DIM))
    d = np.arange(HEAD_DIM)
    axis = d // ROPE_AXIS_DIM
    freq = d % half
    sign = np.where((d % ROPE_AXIS_DIM) < half, -1.0, 1.0).astype(np.float32)
    p = jnp.where(jnp.asarray(axis == 0)[None, :], jnp.asarray(row)[:, None], jnp.asarray(col)[:, None])
    ang = p * inv[freq][None, :]
    cos = jnp.concatenate([jnp.ones((CTX_LEN, HEAD_DIM), F32), jnp.cos(ang)], axis=0)
    sin = jnp.concatenate([jnp.zeros((CTX_LEN, HEAD_DIM), F32), jnp.sin(ang) * sign[None, :]], axis=0)
    return jnp.tile(cos, (1, 2)), jnp.tile(sin, (1, 2))


def _lru_gate_weights(wa, ba, wx, bx):
    def dense(wblk):
        k, bs = wblk.shape[1], wblk.shape[2]
        eye = jnp.eye(k, dtype=wblk.dtype)
        return jnp.einsum("dkij,kl->dkilj", wblk, eye).reshape(2, k * bs, k * bs)
    wg = jnp.concatenate([dense(wa), dense(wx)], axis=-1).astype(BF16)
    bg = jnp.concatenate([ba, bx], axis=-1)[:, None, :].astype(F32)
    return wg, bg


def _layer(x_all, mod, g_mix_pre, g_mix_post, g_ffn_pre, g_ffn_post, w_in, na_rpb, conv_w, conv_b,
           lru_wa, lru_ba, lru_wx, lru_bx, lru_lam, sw_sinks, w_branch, w_out, w_router, b_router,
           w1, b1, w2, b2, layer, with_ctx):
    bsz, s, d = x_all.shape
    t_lat = s - CTX_LEN
    row = lambda v: v.reshape(1, -1).astype(F32)

    cos_t, sin_t = _rope_tables(t_lat)
    qa, ka, va, gb, xb, qs, ks, vs, sg = _inproj(x_all, mod, row(g_mix_pre), _inproj_weight(w_in), cos_t, sin_t)

    ya = _na_attention(qa, ka, va, _na_bias_table(na_rpb, t_lat // GRID_W), with_ctx)

    wg, bg = _lru_gate_weights(lru_wa, lru_ba, lru_wx, lru_bx)
    c8 = (-LRU_C * jax.nn.softplus(-lru_lam.astype(F32)))[:, None, :]
    hfb = _lru(xb, conv_w.astype(F32), row(conv_b), wg, bg, c8)

    g = SW_HEADS // SW_KV_HEADS
    sink_col = jnp.repeat(sw_sinks.astype(F32).reshape(SW_KV_HEADS, g), SW_BLOCK, axis=1)[:, :, None]
    yc = _sw_attention(qs, ks, vs, _sw_mask_table(), sink_col, with_ctx)

    w_r = jnp.zeros((d, LANES), F32).at[:, :N_EXPERTS].set(w_router).astype(BF16)
    b_r = jnp.full((1, LANES), NEG_INF, F32).at[0, :N_EXPERTS].set(b_router.astype(F32))
    x1, h2, idx, wgt, rank, cnt = _merge(
        x_all, ya, hfb, gb, yc, sg, mod, row(g_mix_post), row(g_ffn_pre), w_branch.astype(BF16),
        w_out.astype(BF16), w_r, b_r, jnp.zeros((8, LANES), F32), with_ctx)

    return _moe(h2, idx, wgt, rank, cnt[0, :N_EXPERTS], x1, mod, row(g_ffn_post), w1, b1, w2, b2, layer, with_ctx)


def kernel(x, c, ctx, c_ctx, w_ada, b_ada, g_mix_pre, g_mix_post, g_ffn_pre, g_ffn_post, w_in, na_rpb, conv_w, conv_b, lru_wa, lru_ba, lru_wx, lru_bx, lru_lam, sw_sinks, w_branch, w_out, w_router, b_router, w1, b1, w2, b2):
    depth = w_ada.shape[0]
    bsz, t_lat, d = x.shape
    assert bsz == 2 and d == D_MODEL and ctx.shape[1] == CTX_LEN
    assert t_lat % TILE == 0 and t_lat // GRID_W >= NA_KROWS and t_lat // SW_BLOCK >= 3
    c8 = jnp.zeros((8, d), F32).at[:bsz].set(c).at[bsz].set(c_ctx)
    mods = _ada(c8, w_ada, b_ada)
    x_all = jnp.concatenate([ctx, x], axis=1)
    for l in range(depth):
        with_ctx = l < depth - 1
        out = _layer(x_all, mods[l], g_mix_pre[l], g_mix_post[l], g_ffn_pre[l], g_ffn_post[l], w_in[l], na_rpb[l],
                     conv_w[l], conv_b[l], lru_wa[l], lru_ba[l], lru_wx[l], lru_bx[l], lru_lam[l], sw_sinks[l],
                     w_branch[l], w_out[l], w_router[l], b_router[l], w1, b1, w2, b2, l, with_ctx)
        x_all = out
    return x_all
```

```python
import functools

import numpy as np
import jax
import jax.numpy as jnp
from jax import lax
from jax.experimental import pallas as pl
from jax.experimental.pallas import tpu as pltpu

F32 = jnp.float32
BF16 = jnp.bfloat16
I32 = jnp.int32

D_MODEL = 1024
CTX_LEN = 256
GRID_W = 64
HEAD_DIM = 64
BRANCH_W = 512
N_BRANCH = 3
NA_HEADS = 8
NA_WIN_R = 8
NA_WIN_C = 16
LRU_WIDTH = 512
LRU_BLOCKS = 8
LRU_CONV = 4
LRU_C = 8.0
SW_HEADS = 8
SW_KV_HEADS = 2
SW_WINDOW = 128
SW_BLOCK = 128
ROPE_BASE = 10000.0
ROPE_AXIS_DIM = HEAD_DIM // 2
N_EXPERTS = 32
TOP_K = 4
D_EXPERT = 1024
SWIGLU_LIMIT = 7.0
SWIGLU_ALPHA = 1.702
RMS_EPS = 1e-6
NEG_INF = -1e30

LANES = 128
TILE = 256
NA_QROWS = 4
NA_KROWS = 12
MOE_ROWS = 512
TOK_ROWS = D_MODEL // LANES
DMA_UNROLL = 8
SCAN_UNROLL = 4
VMEM_LIMIT = 56 * 1024 * 1024

_C_QA, _C_KA, _C_VA, _C_GB, _C_XB, _C_QS, _C_QSP, _C_KS, _C_KSP, _C_VS, _C_GL, _C_END = (
    0, 512, 1024, 1536, 2048, 2560, 3072, 3584, 3840, 4096, 4352, 7424)


def _cparams(sem, vmem=VMEM_LIMIT):
    return pltpu.CompilerParams(dimension_semantics=sem, vmem_limit_bytes=vmem)


def _dot(a, b):
    return jnp.dot(a, b, preferred_element_type=F32)


def _dot_t(a, b):
    return lax.dot_general(a, b, (((1,), (1,)), ((), ())), preferred_element_type=F32)


def _rms(x, g):
    return x * lax.rsqrt(jnp.mean(x * x, axis=-1, keepdims=True) + RMS_EPS) * g


def _ada_kernel(c_ref, w_ref, b_ref, o_ref):
    c = c_ref[...]
    s = c * jax.nn.sigmoid(c)
    o_ref[...] = jnp.dot(s, w_ref[...], preferred_element_type=F32, precision=lax.Precision.HIGHEST) + b_ref[...]


def _ada(c8, w_ada, b_ada):
    depth, d, n = w_ada.shape
    tn = 1536
    return pl.pallas_call(
        _ada_kernel,
        grid=(depth, n // tn),
        in_specs=[pl.BlockSpec((8, d), lambda l, j: (0, 0)),
                  pl.BlockSpec((None, d, tn), lambda l, j: (l, 0, j)),
                  pl.BlockSpec((None, 1, tn), lambda l, j: (l, 0, j))],
        out_specs=pl.BlockSpec((None, 8, tn), lambda l, j: (l, 0, j)),
        out_shape=jax.ShapeDtypeStruct((depth, 8, n), F32),
        compiler_params=_cparams(("arbitrary", "arbitrary")),
        name="ada_mod",
    )(c8, w_ada, b_ada.reshape(depth, 1, n))


def _mod_row(mod_ref, b, t, k):
    row = jnp.where(t == 0, 2, b)
    return mod_ref[pl.ds(row, 1), k * D_MODEL:(k + 1) * D_MODEL]


def _inproj_kernel(x_ref, mod_ref, g_ref, w_ref, cos_ref, sin_ref,
                   qa_ref, ka_ref, va_ref, gb_ref, xb_ref, qs_ref, ks_ref, vs_ref, sg_ref):
    b, t = pl.program_id(0), pl.program_id(1)
    x = x_ref[...]
    h = _rms(x, g_ref[...]) * (1.0 + _mod_row(mod_ref, b, t, 1)) + _mod_row(mod_ref, b, t, 0)
    h = h.astype(BF16)

    def seg(lo, hi):
        return _dot(h, w_ref[:, lo:hi])

    qa_ref[...] = seg(_C_QA, _C_KA).astype(BF16)
    ka_ref[...] = seg(_C_KA, _C_VA).astype(BF16)
    va_ref[...] = seg(_C_VA, _C_GB).astype(BF16)
    gb_ref[...] = seg(_C_GB, _C_XB)
    xb_ref[...] = seg(_C_XB, _C_QS)
    cos = cos_ref[...]
    sin = sin_ref[...]
    for j in range(4):
        q = seg(_C_QS + j * LANES, _C_QS + (j + 1) * LANES)
        qp = seg(_C_QSP + j * LANES, _C_QSP + (j + 1) * LANES)
        qs_ref[:, j * LANES:(j + 1) * LANES] = (q * cos + qp * sin).astype(BF16)
    for j in range(2):
        k = seg(_C_KS + j * LANES, _C_KS + (j + 1) * LANES)
        kp = seg(_C_KSP + j * LANES, _C_KSP + (j + 1) * LANES)
        ks_ref[:, j * LANES:(j + 1) * LANES] = (k * cos + kp * sin).astype(BF16)
    vs_ref[...] = seg(_C_VS, _C_GL).astype(BF16)
    for j in range(N_BRANCH):
        lo = _C_GL + j * D_MODEL
        sg_ref[:, j * D_MODEL:(j + 1) * D_MODEL] = jax.nn.sigmoid(seg(lo, lo + D_MODEL))


def _inproj(x_all, mod, g_pre, w_cat, cos_t, sin_t):
    bsz, s, d = x_all.shape
    nt = s // TILE
    tok = lambda b, t: (b, t, 0)
    const2 = lambda b, t: (0, 0)

    def out(width, dtype):
        return jax.ShapeDtypeStruct((bsz, s, width), dtype), pl.BlockSpec((None, TILE, width), tok)

    outs = [out(512, BF16), out(512, BF16), out(512, BF16), out(512, F32), out(512, F32),
            out(512, BF16), out(256, BF16), out(256, BF16), out(N_BRANCH * D_MODEL, F32)]
    return pl.pallas_call(
        _inproj_kernel,
        grid=(bsz, nt),
        in_specs=[pl.BlockSpec((None, TILE, d), tok),
                  pl.BlockSpec((8, 6 * d), const2),
                  pl.BlockSpec((1, d), const2),
                  pl.BlockSpec((d, _C_END), const2),
                  pl.BlockSpec((TILE, LANES), lambda b, t: (t, 0)),
                  pl.BlockSpec((TILE, LANES), lambda b, t: (t, 0))],
        out_specs=[o[1] for o in outs],
        out_shape=[o[0] for o in outs],
        compiler_params=_cparams(("arbitrary", "arbitrary")),
        name="inproj",
    )(x_all, mod, g_pre, w_cat, cos_t, sin_t)


def _two_head_attention(q, score_fn, value_fn):
    lane = lax.broadcasted_iota(I32, (1, LANES), 1)
    low = lane < HEAD_DIM
    outs = []
    for hh in range(2):
        qm = jnp.where(low if hh == 0 else jnp.logical_not(low), q, jnp.zeros_like(q))
        blocks = score_fn(qm, hh)
        m = blocks[0].max(axis=-1, keepdims=True)
        for s in blocks[1:]:
            m = jnp.maximum(m, s.max(axis=-1, keepdims=True))
        ps = [jnp.exp(s - m) for s in blocks]
        l = ps[0].sum(axis=-1, keepdims=True)
        for p in ps[1:]:
            l = l + p.sum(axis=-1, keepdims=True)
        o = value_fn([p.astype(BF16) for p in ps])
        outs.append(o / l)
    return jnp.where(low, outs[0], outs[1])


def _na_kernel(q_ref, k_ref, v_ref, tab_ref, o_ref, *, t_off, t_lat):
    t = pl.program_id(2) + t_off
    q = q_ref[...]
    kc = k_ref[0:CTX_LEN, :]
    vc = v_ref[0:CTX_LEN, :]

    if t_off == 0:
        @pl.when(t == 0)
        def _ctx():
            o = _two_head_attention(q, lambda qm, hh: [_dot_t(qm, kc)], lambda ps: _dot(ps[0], vc))
            o_ref[...] = o.astype(BF16)

    @pl.when(t > 0)
    def _lat():
        nk = NA_KROWS * GRID_W
        start = jnp.clip((t - 2) * TILE, 0, t_lat - nk)
        start = pl.multiple_of(start + CTX_LEN, TILE)
        k = k_ref[pl.ds(start, nk), :]
        v = v_ref[pl.ds(start, nk), :]
        o = _two_head_attention(
            q,
            lambda qm, hh: [_dot_t(qm, k) + tab_ref[hh], _dot_t(qm, kc)],
            lambda ps: _dot(ps[0], v) + _dot(ps[1], vc))
        o_ref[...] = o.astype(BF16)


def _na_attention(qa, ka, va, table, with_ctx):
    bsz, s, _ = qa.shape
    t_lat = s - CTX_LEN
    nrb = t_lat // TILE
    t_off = 0 if with_ctx else 1
    steps = nrb + 1 - t_off
    out_rows = steps * TILE

    def var(i):
        rb = jnp.maximum(i + t_off - 1, 0)
        return jnp.where(rb == 0, 0, jnp.where(rb == nrb - 1, 2, 1))

    return pl.pallas_call(
        functools.partial(_na_kernel, t_off=t_off, t_lat=t_lat),
        grid=(bsz, NA_HEADS // 2, steps),
        in_specs=[pl.BlockSpec((None, TILE, LANES), lambda b, p, i: (b, i + t_off, p)),
                  pl.BlockSpec((None, s, LANES), lambda b, p, i: (b, 0, p)),
                  pl.BlockSpec((None, s, LANES), lambda b, p, i: (b, 0, p)),
                  pl.BlockSpec((None, 2, TILE, NA_KROWS * GRID_W), lambda b, p, i: (var(i), p, 0, 0))],
        out_specs=pl.BlockSpec((None, TILE, LANES), lambda b, p, i: (b, i, p)),
        out_shape=jax.ShapeDtypeStruct((bsz, out_rows, BRANCH_W), BF16),
        compiler_params=_cparams(("arbitrary", "arbitrary", "arbitrary")),
        name="na_attention",
    )(qa, ka, va, table)


def _na_bias_table(rpb, rows):
    qr = np.arange(NA_QROWS)
    kr = np.arange(NA_KROWS)
    r0 = np.array([0, NA_WIN_R // 2, rows - NA_QROWS])
    kr0 = np.array([0, 0, rows - NA_KROWS])
    q_abs = r0[:, None] + qr[None, :]
    k_abs = kr0[:, None] + kr[None, :]
    rs = np.clip(q_abs - NA_WIN_R // 2, 0, rows - NA_WIN_R)
    rvalid = (k_abs[:, None, :] >= rs[:, :, None]) & (k_abs[:, None, :] < rs[:, :, None] + NA_WIN_R)
    ridx = np.clip(k_abs[:, None, :] - q_abs[:, :, None] + NA_WIN_R - 1, 0, 2 * NA_WIN_R - 2)
    qc = np.arange(GRID_W)
    kc = np.arange(GRID_W)
    q_start = np.clip(qc - NA_WIN_C // 2, 0, GRID_W - NA_WIN_C)
    cvalid = (kc[None, :] >= q_start[:, None]) & (kc[None, :] < q_start[:, None] + NA_WIN_C)
    cidx = np.clip(kc[None, :] - qc[:, None] + NA_WIN_C - 1, 0, 2 * NA_WIN_C - 2)
    nr, ncol = 2 * NA_WIN_R - 1, 2 * NA_WIN_C - 1
    rsel = ((ridx[..., None] == np.arange(nr)) & rvalid[..., None]).astype(np.float32)
    csel = ((cidx[..., None] == np.arange(ncol)) & cvalid[..., None]).astype(np.float32)
    hp = lax.Precision.HIGHEST
    part = jnp.einsum("hij,vqki->hvqkj", rpb.astype(F32), jnp.asarray(rsel), precision=hp)
    tab = jnp.einsum("hvqkj,cdj->vhqckd", part, jnp.asarray(csel), precision=hp)
    rmask = jnp.asarray(np.where(rvalid, 0.0, NEG_INF).astype(np.float32))
    cmask = jnp.asarray(np.where(cvalid, 0.0, NEG_INF).astype(np.float32))
    tab = tab + rmask[:, None, :, None, :, None] + cmask[None, None, None, :, None, :]
    return tab.reshape(3, NA_HEADS, NA_QROWS * GRID_W, NA_KROWS * GRID_W)


def _sw_kernel(q_ref, k_ref, v_ref, mask_ref, sink_ref, o_ref, *, t_off, t_lat):
    t = pl.program_id(2) + t_off
    q = q_ref[...]
    lane = lax.broadcasted_iota(I32, (1, LANES), 1)
    low = lane < HEAD_DIM
    zero = jnp.zeros((SW_BLOCK, LANES), BF16)
    q01, q23 = q[:, :LANES], q[:, LANES:]
    qs = jnp.concatenate([jnp.where(low, q01, zero), jnp.where(low, zero, q01),
                          jnp.where(low, q23, zero), jnp.where(low, zero, q23)], axis=0)
    kc = k_ref[0:CTX_LEN, :]
    vc = v_ref[0:CTX_LEN, :]
    sink = sink_ref[...]
    s_cx = _dot_t(qs, kc)

    def finish(blocks, vals):
        m = sink
        for s in blocks:
            m = jnp.maximum(m, s.max(axis=-1, keepdims=True))
        ps = [jnp.exp(s - m) for s in blocks]
        l = jnp.exp(sink - m)
        for p in ps:
            l = l + p.sum(axis=-1, keepdims=True)
        o = _dot(ps[0].astype(BF16), vals[0])
        for p, v in zip(ps[1:], vals[1:]):
            o = o + _dot(p.astype(BF16), v)
        o = o / l
        b = SW_BLOCK
        o_ref[:, :LANES] = jnp.where(low, o[0:b], o[b:2 * b]).astype(BF16)
        o_ref[:, LANES:] = jnp.where(low, o[2 * b:3 * b], o[3 * b:4 * b]).astype(BF16)

    n_ctx_tiles = CTX_LEN // SW_BLOCK
    if t_off == 0:
        @pl.when(t < n_ctx_tiles)
        def _ctx():
            finish([s_cx], [vc])

    @pl.when(t >= n_ctx_tiles)
    def _lat():
        nk = 3 * SW_BLOCK
        start = jnp.clip((t - n_ctx_tiles - 1) * SW_BLOCK, 0, t_lat - nk)
        start = pl.multiple_of(start + CTX_LEN, SW_BLOCK)
        k = k_ref[pl.ds(start, nk), :]
        v = v_ref[pl.ds(start, nk), :]
        finish([_dot_t(qs, k) + mask_ref[...], s_cx], [v, vc])


def _sw_attention(qs, ks, vs, mask, sink_col, with_ctx):
    bsz, s, _ = qs.shape
    t_lat = s - CTX_LEN
    nb = t_lat // SW_BLOCK
    n_ctx_tiles = CTX_LEN // SW_BLOCK
    t_off = 0 if with_ctx else n_ctx_tiles
    steps = nb + n_ctx_tiles - t_off

    def var(i):
        n = jnp.maximum(i + t_off - n_ctx_tiles, 0)
        return jnp.where(n == 0, 0, jnp.where(n == nb - 1, 2, 1))

    g = SW_HEADS // SW_KV_HEADS
    return pl.pallas_call(
        functools.partial(_sw_kernel, t_off=t_off, t_lat=t_lat),
        grid=(bsz, SW_KV_HEADS, steps),
        in_specs=[pl.BlockSpec((None, SW_BLOCK, 2 * LANES), lambda b, h, i: (b, i + t_off, h)),
                  pl.BlockSpec((None, s, LANES), lambda b, h, i: (b, 0, h)),
                  pl.BlockSpec((None, s, LANES), lambda b, h, i: (b, 0, h)),
                  pl.BlockSpec((None, g * SW_BLOCK, 3 * SW_BLOCK), lambda b, h, i: (var(i), 0, 0)),
                  pl.BlockSpec((None, g * SW_BLOCK, 1), lambda b, h, i: (h, 0, 0))],
        out_specs=pl.BlockSpec((None, SW_BLOCK, 2 * LANES), lambda b, h, i: (b, i, h)),
        out_shape=jax.ShapeDtypeStruct((bsz, steps * SW_BLOCK, BRANCH_W), BF16),
        compiler_params=_cparams(("arbitrary", "arbitrary", "arbitrary")),
        name="sw_attention",
    )(qs, ks, vs, mask, sink_col)


def _sw_mask_table():
    i = np.arange(SW_BLOCK)[:, None]
    j = np.arange(3 * SW_BLOCK)[None, :]
    tabs = []
    for shift in (0, SW_BLOCK, 2 * SW_BLOCK):
        rel = j - i - shift
        tabs.append(np.where(np.abs(rel) <= SW_WINDOW, 0.0, NEG_INF).astype(np.float32))
    tab = np.stack(tabs)
    return jnp.asarray(np.tile(tab, (1, SW_HEADS // SW_KV_HEADS, 1)))


def _lru_kernel(x_ref, xp_ref, xn_ref, cw_ref, cb_ref, wg_ref, bg_ref, c8_ref, o_ref,
                xs, a_s, b_s, carry, *, nc):
    d, i = pl.program_id(1), pl.program_id(2)
    ci = jnp.where(d == 0, i, jnp.where(i == 0, 0, nc - i))
    tc = TILE
    seg = tc // 8
    w = LRU_WIDTH

    no_prev = (ci == 0) | (ci == 1)
    no_next = (ci == 0) | (ci == nc - 1)
    xs[0:8, :] = jnp.where(no_prev, 0.0, xp_ref[...])
    xs[8:8 + tc, :] = x_ref[...]
    xs[8 + tc:16 + tc, :] = jnp.where(no_next, 0.0, xn_ref[...])
    u = cb_ref[...] + cw_ref[0:1, :] * xs[7:7 + tc, :]
    for j in range(1, LRU_CONV):
        u = u + cw_ref[j:j + 1, :] * xs[7 + j:7 + j + tc, :]

    g = _dot(u.astype(BF16), wg_ref[...]) + bg_ref[...]
    r = jax.nn.sigmoid(g[:, :w])
    ig = jax.nn.sigmoid(g[:, w:])
    log_a = c8_ref[...] * r
    a = jnp.exp(log_a)
    th = jnp.tanh(log_a)
    bb = jnp.sqrt(-2.0 * th / (1.0 - th)) * (ig * u)
    nl = w // LANES
    for cidx in range(nl):
        a_s[cidx] = a[:, cidx * LANES:(cidx + 1) * LANES]
        b_s[cidx] = bb[:, cidx * LANES:(cidx + 1) * LANES]

    @pl.when(i == 0)
    def _init():
        carry[...] = jnp.zeros_like(carry)

    def body(jj, hp):
        hs, ps = hp
        j = jnp.where(d == 0, jj, seg - 1 - jj)
        hs_new, ps_new = [], []
        for cidx in range(nl):
            at = a_s[cidx, pl.ds(j, 8, stride=seg), :]
            bt = b_s[cidx, pl.ds(j, 8, stride=seg), :]
            h = at * hs[cidx] + bt
            p = at * ps[cidx]
            b_s[cidx, pl.ds(j, 8, stride=seg), :] = h
            a_s[cidx, pl.ds(j, 8, stride=seg), :] = p
            hs_new.append(h)
            ps_new.append(p)
        return tuple(hs_new), tuple(ps_new)

    init = (tuple(jnp.zeros((8, LANES), F32) for _ in range(nl)), tuple(jnp.ones((8, LANES), F32) for _ in range(nl)))
    hs_end, ps_end = lax.fori_loop(0, seg, body, init, unroll=SCAN_UNROLL)
    h_end = jnp.concatenate(hs_end, axis=1)
    p_end = jnp.concatenate(ps_end, axis=1)

    c_in = carry[0:1, :]

    def fix(sidx, c):
        rows = slice(sidx * seg, (sidx + 1) * seg)
        for cidx in range(nl):
            lanes = slice(cidx * LANES, (cidx + 1) * LANES)
            o_ref[rows, lanes] = b_s[cidx, rows, :] + a_s[cidx, rows, :] * c[:, lanes]
        return h_end[sidx:sidx + 1, :] + p_end[sidx:sidx + 1, :] * c

    @pl.when(d == 0)
    def _fwd():
        c = c_in
        for sidx in range(8):
            c = fix(sidx, c)
        carry[0:1, :] = c

    @pl.when(d == 1)
    def _bwd():
        c = c_in
        for sidx in range(7, -1, -1):
            c = fix(sidx, c)
        carry[0:1, :] = c


def _lru(xb, conv_w, conv_b, wg, bg, c8):
    bsz, s, w = xb.shape
    nc = s // TILE
    r8 = TILE // 8

    def chunk(d, i):
        return jnp.where(d == 0, i, jnp.where(i == 0, 0, nc - i))

    return pl.pallas_call(
        functools.partial(_lru_kernel, nc=nc),
        grid=(bsz, 2, nc),
        in_specs=[pl.BlockSpec((None, TILE, w), lambda b, d, i: (b, chunk(d, i), 0)),
                  pl.BlockSpec((None, 8, w), lambda b, d, i: (b, jnp.maximum(chunk(d, i) * r8 - 1, 0), 0)),
                  pl.BlockSpec((None, 8, w), lambda b, d, i: (b, jnp.minimum((chunk(d, i) + 1) * r8, s // 8 - 1), 0)),
                  pl.BlockSpec((LRU_CONV, w), lambda b, d, i: (0, 0)),
                  pl.BlockSpec((1, w), lambda b, d, i: (0, 0)),
                  pl.BlockSpec((None, w, 2 * w), lambda b, d, i: (d, 0, 0)),
                  pl.BlockSpec((None, 1, 2 * w), lambda b, d, i: (d, 0, 0)),
                  pl.BlockSpec((None, 1, w), lambda b, d, i: (d, 0, 0))],
        out_specs=pl.BlockSpec((None, None, TILE, w), lambda b, d, i: (d, b, chunk(d, i), 0)),
        out_shape=jax.ShapeDtypeStruct((2, bsz, s, w), F32),
        scratch_shapes=[pltpu.VMEM((TILE + 16, w), F32), pltpu.VMEM((w // LANES, TILE, LANES), F32),
                        pltpu.VMEM((w // LANES, TILE, LANES), F32), pltpu.VMEM((8, w), F32)],
        compiler_params=_cparams(("arbitrary", "arbitrary", "arbitrary")),
        name="rglru",
    )(xb, xb, xb, conv_w, conv_b, wg, bg, c8)


def _gelu_tanh(x):
    return 0.5 * x * (1.0 + jnp.tanh(np.sqrt(2.0 / np.pi).astype(np.float32) * (x + 0.044715 * (x * x * x))))


def _merge_kernel(x_ref, ya_ref, hf_ref, hb_ref, gb_ref, yc_ref, sg_ref, mod_ref, gpost_ref, gpre_ref,
                  wbr_ref, wout_ref, wr_ref, br_ref, cnt0_ref,
                  x1_ref, h2_ref, idx_ref, wgt_ref, rank_ref, cnt_ref, cnt_s, *, t_off):
    b, ti = pl.program_id(0), pl.program_id(1)
    t = ti + t_off
    d = D_MODEL

    @pl.when((b == 0) & (ti == 0))
    def _init():
        cnt_s[...] = cnt0_ref[...]

    yb = (_gelu_tanh(gb_ref[...]) * (hf_ref[...] + hb_ref[...])).astype(BF16)
    ys = (ya_ref[...], yb, yc_ref[...])
    mix = sg_ref[:, 0:d] * _dot(ys[0], wbr_ref[0])
    for k in range(1, N_BRANCH):
        mix = mix + sg_ref[:, k * d:(k + 1) * d] * _dot(ys[k], wbr_ref[k])
    y = _dot(mix.astype(BF16), wout_ref[...])
    x1 = x_ref[...] + _mod_row(mod_ref, b, t, 2) * _rms(y, gpost_ref[...])
    x1_ref[...] = x1
    h2 = _rms(x1, gpre_ref[...]) * (1.0 + _mod_row(mod_ref, b, t, 4)) + _mod_row(mod_ref, b, t, 3)
    _store_token_rows(h2_ref, h2, TILE)

    logits = _dot(h2.astype(BF16), wr_ref[...]) + br_ref[...]
    lane = lax.broadcasted_iota(I32, (TILE, LANES), 1)
    work = logits
    tops, sels = [], []
    for k in range(TOP_K):
        m = work.max(axis=-1, keepdims=True)
        first = jnp.where(work == m, lane, LANES).min(axis=-1, keepdims=True)
        sel = lane == first
        work = jnp.where(sel, -3.0e38, work)
        tops.append(m)
        sels.append(sel)
    es = [jnp.exp(m - tops[0]) for m in tops]
    den = es[0] + es[1] + es[2] + es[3]

    chosen = jnp.zeros((TILE, LANES), F32)
    for sel in sels:
        chosen = jnp.where(sel, 1.0, chosen)
    rr = lax.broadcasted_iota(I32, (TILE, TILE), 0)
    cc = lax.broadcasted_iota(I32, (TILE, TILE), 1)
    tri = jnp.where(cc < rr, 1.0, 0.0).astype(BF16)
    pref = _dot(tri, chosen.astype(BF16)) + cnt_s[0:1, :]
    cnt_new = cnt_s[0:1, :] + chosen.sum(axis=0, keepdims=True)
    cnt_s[0:1, :] = cnt_new
    cnt_ref[...] = jnp.broadcast_to(cnt_new, cnt_ref.shape)

    idx_o = jnp.zeros((TILE, LANES), I32)
    wgt_o = jnp.zeros((TILE, LANES), F32)
    rank_o = jnp.zeros((TILE, LANES), I32)
    for k in range(TOP_K):
        first = jnp.where(sels[k], lane, 0).max(axis=-1, keepdims=True)
        rk = jnp.where(sels[k], pref, 0.0).sum(axis=-1, keepdims=True)
        idx_o = jnp.where(lane == k, first, idx_o)
        wgt_o = jnp.where(lane == k, es[k] / den, wgt_o)
        rank_o = jnp.where(lane == k, rk.astype(I32), rank_o)
    idx_ref[...] = idx_o
    wgt_ref[...] = wgt_o
    rank_ref[...] = rank_o


def _merge(x_all, ya, hfb, gb, yc, sg, mod, g_post, g_pre, w_br, w_out, w_r, b_r, cnt0, with_ctx):
    bsz, s, d = x_all.shape
    t_off = 0 if with_ctx else 1
    nt = s // TILE - t_off
    rows = nt * TILE
    inp = lambda b, t: (b, t + t_off, 0)
    loc = lambda b, t: (b, t, 0)
    c2 = lambda b, t: (0, 0)
    c3 = lambda b, t: (0, 0, 0)

    def out(width, dtype):
        return jax.ShapeDtypeStruct((bsz, rows, width), dtype), pl.BlockSpec((None, TILE, width), loc)

    h2_out = (jax.ShapeDtypeStruct((bsz, rows * TOK_ROWS, LANES), F32),
              pl.BlockSpec((None, TILE * TOK_ROWS, LANES), loc))
    outs = [out(d, F32), h2_out, out(LANES, I32), out(LANES, F32), out(LANES, I32)]
    shapes = [o[0] for o in outs] + [jax.ShapeDtypeStruct((8, LANES), F32)]
    specs = [o[1] for o in outs] + [pl.BlockSpec((8, LANES), c2)]
    return pl.pallas_call(
        functools.partial(_merge_kernel, t_off=t_off),
        grid=(bsz, nt),
        in_specs=[pl.BlockSpec((None, TILE, d), inp),
                  pl.BlockSpec((None, TILE, BRANCH_W), loc),
                  pl.BlockSpec((None, None, TILE, BRANCH_W), lambda b, t: (0, b, t + t_off, 0)),
                  pl.BlockSpec((None, None, TILE, BRANCH_W), lambda b, t: (1, b, t + t_off, 0)),
                  pl.BlockSpec((None, TILE, BRANCH_W), inp),
                  pl.BlockSpec((None, TILE, BRANCH_W), loc),
                  pl.BlockSpec((None, TILE, N_BRANCH * d), inp),
                  pl.BlockSpec((8, 6 * d), c2),
                  pl.BlockSpec((1, d), c2),
                  pl.BlockSpec((1, d), c2),
                  pl.BlockSpec((N_BRANCH, BRANCH_W, d), c3),
                  pl.BlockSpec((d, d), c2),
                  pl.BlockSpec((d, LANES), c2),
                  pl.BlockSpec((1, LANES), c2),
                  pl.BlockSpec((8, LANES), c2)],
        out_specs=specs,
        out_shape=shapes,
        scratch_shapes=[pltpu.VMEM((8, LANES), F32)],
        compiler_params=_cparams(("arbitrary", "arbitrary")),
        name="merge_router",
    )(x_all, ya, hfb, hfb, gb, yc, sg, mod, g_post, g_pre, w_br, w_out, w_r, b_r, cnt0)


def _store_token_rows(ref, val, n, lead=()):
    for j in range(TOK_ROWS):
        ref[lead + (pl.ds(j, n, stride=TOK_ROWS), slice(None))] = val[:, j * LANES:(j + 1) * LANES]


def _load_token_rows(ref, n, lead=()):
    return jnp.concatenate([ref[lead + (pl.ds(j, n, stride=TOK_ROWS), slice(None))] for j in range(TOK_ROWS)], axis=1)


def _token_slice(i):
    return pl.ds(pl.multiple_of(i * TOK_ROWS, TOK_ROWS), TOK_ROWS)


def _dispatch_kernel(pos_ref, h_ref, xs_in_ref, xs_ref, sem):
    del xs_in_ref

    def row_copy(i, k):
        p = pos_ref[0, 0, i * TOP_K + k]
        return pltpu.make_async_copy(h_ref.at[_token_slice(i)], xs_ref.at[_token_slice(p)], sem)

    def start(i, c):
        for k in range(TOP_K):
            row_copy(i, k).start()
        return c

    lax.fori_loop(0, TILE, start, 0, unroll=DMA_UNROLL)
    for _ in range(TOP_K):
        pltpu.make_async_copy(h_ref, xs_ref.at[pl.ds(0, TILE * TOK_ROWS)], sem).wait()


def _dispatch(h2, pos, n_rows):
    bsz, rows = pos.shape[:2]
    nt = rows // TILE
    zeros = jnp.zeros((n_rows * TOK_ROWS, LANES), F32)
    return pl.pallas_call(
        _dispatch_kernel,
        grid=(bsz, nt),
        in_specs=[pl.BlockSpec((1, 1, TILE * TOP_K), lambda b, t: (b * nt + t, 0, 0), memory_space=pltpu.SMEM),
                  pl.BlockSpec((TILE * TOK_ROWS, LANES), lambda b, t: (b * nt + t, 0)),
                  pl.BlockSpec(memory_space=pl.ANY)],
        out_specs=pl.BlockSpec(memory_space=pl.ANY),
        out_shape=jax.ShapeDtypeStruct((n_rows * TOK_ROWS, LANES), F32),
        scratch_shapes=[pltpu.SemaphoreType.DMA(())],
        input_output_aliases={2: 0},
        compiler_params=_cparams(("arbitrary", "arbitrary")),
        name="moe_dispatch",
    )(pos.reshape(bsz * nt, 1, TILE * TOP_K), h2.reshape(bsz * rows * TOK_ROWS, LANES), zeros)


def _expert_kernel(te_ref, nu_ref, x_ref, w1_ref, b1_ref, w2_ref, b2_ref, y_ref, w1_s, w2_s):
    t = pl.program_id(0)
    prev = te_ref[jnp.maximum(t - 1, 0)]
    active = t < nu_ref[0]

    @pl.when(active & ((t == 0) | (te_ref[t] != prev)))
    def _load():
        w1_s[...] = w1_ref[...].astype(BF16)
        w2_s[...] = w2_ref[...].astype(BF16)

    @pl.when(active)
    def _run():
        f = D_EXPERT
        x = _load_token_rows(x_ref, MOE_ROWS).astype(BF16)
        hid = _dot(x, w1_s[...]) + b1_ref[...]
        glu = jnp.minimum(hid[:, :f], SWIGLU_LIMIT)
        lin = jnp.clip(hid[:, f:], -SWIGLU_LIMIT, SWIGLU_LIMIT)
        act = glu * jax.nn.sigmoid(SWIGLU_ALPHA * glu) * (lin + 1.0)
        _store_token_rows(y_ref, _dot(act.astype(BF16), w2_s[...]) + b2_ref[...], MOE_ROWS)

    @pl.when(jnp.logical_not(active))
    def _idle():
        y_ref[...] = jnp.zeros_like(y_ref)


def _experts(xs, tile_expert, n_used, w1, b1, w2, b2, layer):
    depth, e, d, f2 = w1.shape
    n_tiles = xs.shape[0] // (MOE_ROWS * TOK_ROWS)
    blk = (MOE_ROWS * TOK_ROWS, LANES)
    xmap = lambda t, te, nu: (jnp.minimum(t, nu[0] - 1), 0)
    wmap = lambda t, te, nu: (layer, te[t], 0, 0)
    grid_spec = pltpu.PrefetchScalarGridSpec(
        num_scalar_prefetch=2,
        grid=(n_tiles,),
        in_specs=[pl.BlockSpec(blk, xmap),
                  pl.BlockSpec((None, None, d, f2), wmap),
                  pl.BlockSpec((None, None, 1, f2), wmap),
                  pl.BlockSpec((None, None, f2 // 2, d), wmap),
                  pl.BlockSpec((None, None, 1, d), wmap)],
        out_specs=pl.BlockSpec(blk, lambda t, te, nu: (t, 0)),
        scratch_shapes=[pltpu.VMEM((d, f2), BF16), pltpu.VMEM((f2 // 2, d), BF16)])
    return pl.pallas_call(
        _expert_kernel,
        grid_spec=grid_spec,
        out_shape=jax.ShapeDtypeStruct(xs.shape, F32),
        compiler_params=_cparams(("arbitrary",)),
        name="moe_experts",
    )(tile_expert, n_used, xs, w1, b1.reshape(depth, e, 1, f2), w2, b2.reshape(depth, e, 1, d))


def _combine_kernel(pos_ref, ys_ref, wgt_ref, x1_ref, mod_ref, gpost_ref, o_ref, buf, sem, *, t_off):
    b, t = pl.program_id(0), pl.program_id(1) + t_off

    def row_copy(i, k):
        p = pos_ref[0, 0, i * TOP_K + k]
        return pltpu.make_async_copy(ys_ref.at[_token_slice(p)], buf.at[k, _token_slice(i)], sem)

    def start(i, c):
        for k in range(TOP_K):
            row_copy(i, k).start()
        return c

    lax.fori_loop(0, TILE, start, 0, unroll=DMA_UNROLL)
    for k in range(TOP_K):
        pltpu.make_async_copy(ys_ref.at[pl.ds(0, TILE * TOK_ROWS)], buf.at[k], sem).wait()

    wgt = wgt_ref[...]
    moe = wgt[:, 0:1] * _load_token_rows(buf, TILE, (0,))
    for k in range(1, TOP_K):
        moe = moe + wgt[:, k:k + 1] * _load_token_rows(buf, TILE, (k,))
    o_ref[...] = x1_ref[...] + _mod_row(mod_ref, b, t, 5) * _rms(moe, gpost_ref[...])


def _combine(ys, pos, wgt, x1, mod, g_post, with_ctx):
    bsz, rows, d = x1.shape
    nt = rows // TILE
    tok = lambda b, t: (b, t, 0)
    return pl.pallas_call(
        functools.partial(_combine_kernel, t_off=0 if with_ctx else 1),
        grid=(bsz, nt),
        in_specs=[pl.BlockSpec((1, 1, TILE * TOP_K), lambda b, t: (b * nt + t, 0, 0), memory_space=pltpu.SMEM),
                  pl.BlockSpec(memory_space=pl.ANY),
                  pl.BlockSpec((None, TILE, LANES), tok),
                  pl.BlockSpec((None, TILE, d), tok),
                  pl.BlockSpec((8, 6 * d), lambda b, t: (0, 0)),
                  pl.BlockSpec((1, d), lambda b, t: (0, 0))],
        out_specs=pl.BlockSpec((None, TILE, d), tok),
        out_shape=jax.ShapeDtypeStruct((bsz, rows, d), F32),
        scratch_shapes=[pltpu.VMEM((TOP_K, TILE * TOK_ROWS, LANES), F32), pltpu.SemaphoreType.DMA(())],
        compiler_params=_cparams(("arbitrary", "arbitrary")),
        name="moe_combine",
    )(pos.reshape(bsz * nt, 1, TILE * TOP_K), ys, wgt, x1, mod, g_post)


def _moe(h2, idx, wgt, rank, counts, x1, mod, g_post, w1, b1, w2, b2, layer, with_ctx):
    bsz, rows, d = x1.shape
    n_tiles = (bsz * rows * TOP_K) // MOE_ROWS + N_EXPERTS
    n_rows = n_tiles * MOE_ROWS
    cnt = counts.astype(I32)
    tiles_e = (cnt + MOE_ROWS - 1) // MOE_ROWS
    tile_end = jnp.cumsum(tiles_e)
    row_start = (tile_end - tiles_e) * MOE_ROWS
    n_used = tile_end[-1:]
    tile_ids = jnp.minimum(jnp.arange(n_tiles, dtype=I32), n_used[0] - 1)
    tile_expert = jnp.sum((tile_end[None, :] <= tile_ids[:, None]).astype(I32), axis=1)
    tile_expert = jnp.minimum(tile_expert, N_EXPERTS - 1)
    experts = jnp.arange(N_EXPERTS, dtype=I32)
    start_of = jnp.sum(jnp.where(idx[..., :TOP_K, None] == experts, row_start, 0), axis=-1)
    pos = (start_of + rank[..., :TOP_K]).astype(I32)
    xs = _dispatch(h2, pos, n_rows)
    ys = _experts(xs, tile_expert, n_used.astype(I32), w1, b1, w2, b2, layer)
    return _combine(ys, pos, wgt, x1, mod, g_post, with_ctx)


def _rope_partner(w):
    half = ROPE_AXIS_DIM // 2
    return w.reshape(w.shape[0], -1, 2, half)[:, :, ::-1, :].reshape(w.shape)


def _dup_heads(w):
    d = w.shape[0]
    w = w.reshape(d, -1, 1, HEAD_DIM)
    return jnp.broadcast_to(w, (d, w.shape[1], 2, HEAD_DIM)).reshape(d, -1)


def _inproj_weight(w_in):
    sizes = (BRANCH_W, BRANCH_W, BRANCH_W, LRU_WIDTH, LRU_WIDTH, BRANCH_W,
             SW_KV_HEADS * HEAD_DIM, SW_KV_HEADS * HEAD_DIM, N_BRANCH * D_MODEL)
    offs = np.cumsum((0,) + sizes)
    qa, ka, va, gb, xb, qs, ks, vs, gl = [w_in[:, offs[i]:offs[i + 1]] for i in range(9)]
    scale = HEAD_DIM ** -0.5
    qs = qs * scale
    ksd = _dup_heads(ks)
    cols = [qa * scale, ka, va, gb, xb, qs, _rope_partner(qs), ksd, _rope_partner(ksd), _dup_heads(vs), gl]
    return jnp.concatenate(cols, axis=1).astype(BF16)


def _rope_tables(t_lat):
    pos = np.arange(t_lat)
    row = (pos // GRID_W).astype(np.float32)
    col = (pos % GRID_W).astype(np.float32)
    half = ROPE_AXIS_DIM // 2
    inv = (ROPE_BASE ** (-jnp.arange(0, ROPE_AXIS_DIM, 2, dtype=F32) / ROPE_AXIS_DIM))
    d = np.arange(HEAD_DIM)
    axis = d // ROPE_AXIS_DIM
    freq = d % half
    sign = np.where((d % ROPE_AXIS_DIM) < half, -1.0, 1.0).astype(np.float32)
    p = jnp.where(jnp.asarray(axis == 0)[None, :], jnp.asarray(row)[:, None], jnp.asarray(col)[:, None])
    ang = p * inv[freq][None, :]
    cos = jnp.concatenate([jnp.ones((CTX_LEN, HEAD_DIM), F32), jnp.cos(ang)], axis=0)
    sin = jnp.concatenate([jnp.zeros((CTX_LEN, HEAD_DIM), F32), jnp.sin(ang) * sign[None, :]], axis=0)
    return jnp.tile(cos, (1, 2)), jnp.tile(sin, (1, 2))


def _lru_gate_weights(wa, ba, wx, bx):
    def dense(wblk):
        k, bs = wblk.shape[1], wblk.shape[2]
        eye = jnp.eye(k, dtype=wblk.dtype)
        return jnp.einsum("dkij,kl->dkilj", wblk, eye).reshape(2, k * bs, k * bs)
    wg = jnp.concatenate([dense(wa), dense(wx)], axis=-1).astype(BF16)
    bg = jnp.concatenate([ba, bx], axis=-1)[:, None, :].astype(F32)
    return wg, bg


def _layer(x_all, mod, g_mix_pre, g_mix_post, g_ffn_pre, g_ffn_post, w_in, na_rpb, conv_w, conv_b,
           lru_wa, lru_ba, lru_wx, lru_bx, lru_lam, sw_sinks, w_branch, w_out, w_router, b_router,
           w1, b1, w2, b2, layer, with_ctx):
    bsz, s, d = x_all.shape
    t_lat = s - CTX_LEN
    row = lambda v: v.reshape(1, -1).astype(F32)

    cos_t, sin_t = _rope_tables(t_lat)
    qa, ka, va, gb, xb, qs, ks, vs, sg = _inproj(x_all, mod, row(g_mix_pre), _inproj_weight(w_in), cos_t, sin_t)

    ya = _na_attention(qa, ka, va, _na_bias_table(na_rpb, t_lat // GRID_W), with_ctx)

    wg, bg = _lru_gate_weights(lru_wa, lru_ba, lru_wx, lru_bx)
    c8 = (-LRU_C * jax.nn.softplus(-lru_lam.astype(F32)))[:, None, :]
    hfb = _lru(xb, conv_w.astype(F32), row(conv_b), wg, bg, c8)

    g = SW_HEADS // SW_KV_HEADS
    sink_col = jnp.repeat(sw_sinks.astype(F32).reshape(SW_KV_HEADS, g), SW_BLOCK, axis=1)[:, :, None]
    yc = _sw_attention(qs, ks, vs, _sw_mask_table(), sink_col, with_ctx)

    w_r = jnp.zeros((d, LANES), F32).at[:, :N_EXPERTS].set(w_router).astype(BF16)
    b_r = jnp.full((1, LANES), NEG_INF, F32).at[0, :N_EXPERTS].set(b_router.astype(F32))
    x1, h2, idx, wgt, rank, cnt = _merge(
        x_all, ya, hfb, gb, yc, sg, mod, row(g_mix_post), row(g_ffn_pre), w_branch.astype(BF16),
        w_out.astype(BF16), w_r, b_r, jnp.zeros((8, LANES), F32), with_ctx)

    return _moe(h2, idx, wgt, rank, cnt[0, :N_EXPERTS], x1, mod, row(g_ffn_post), w1, b1, w2, b2, layer, with_ctx)


def kernel(x, c, ctx, c_ctx, w_ada, b_ada, g_mix_pre, g_mix_post, g_ffn_pre, g_ffn_post, w_in, na_rpb, conv_w, conv_b, lru_wa, lru_ba, lru_wx, lru_bx, lru_lam, sw_sinks, w_branch, w_out, w_router, b_router, w1, b1, w2, b2):
    depth = w_ada.shape[0]
    bsz, t_lat, d = x.shape
    assert bsz == 2 and d == D_MODEL and ctx.shape[1] == CTX_LEN
    assert t_lat % TILE == 0 and t_lat // GRID_W >= NA_KROWS and t_lat // SW_BLOCK >= 3
    c8 = jnp.zeros((8, d), F32).at[:bsz].set(c).at[bsz].set(c_ctx)
    mods = _ada(c8, w_ada, b_ada)
    x_all = jnp.concatenate([ctx, x], axis=1)
    for l in range(depth):
        with_ctx = l < depth - 1
        out = _layer(x_all, mods[l], g_mix_pre[l], g_mix_post[l], g_ffn_pre[l], g_ffn_post[l], w_in[l], na_rpb[l],
                     conv_w[l], conv_b[l], lru_wa[l], lru_ba[l], lru_wx[l], lru_bx[l], lru_lam[l], sw_sinks[l],
                     w_branch[l], w_out[l], w_router[l], b_router[l], w1, b1, w2, b2, l, with_ctx)
        x_all = out
    return x_all
```

```python
import functools

import numpy as np
import jax
import jax.numpy as jnp
from jax import lax
from jax.experimental import pallas as pl
from jax.experimental.pallas import tpu as pltpu

F32 = jnp.float32
BF16 = jnp.bfloat16
I32 = jnp.int32

D_MODEL = 1024
CTX_LEN = 256
GRID_W = 64
HEAD_DIM = 64
BRANCH_W = 512
N_BRANCH = 3
NA_HEADS = 8
NA_WIN_R = 8
NA_WIN_C = 16
LRU_WIDTH = 512
LRU_BLOCKS = 8
LRU_CONV = 4
LRU_C = 8.0
SW_HEADS = 8
SW_KV_HEADS = 2
SW_WINDOW = 128
SW_BLOCK = 128
ROPE_BASE = 10000.0
ROPE_AXIS_DIM = HEAD_DIM // 2
N_EXPERTS = 32
TOP_K = 4
D_EXPERT = 1024
SWIGLU_LIMIT = 7.0
SWIGLU_ALPHA = 1.702
RMS_EPS = 1e-6
NEG_INF = -1e30

LANES = 128
TILE = 256
NA_QROWS = 4
NA_KROWS = 12
MOE_ROWS = 512
TOK_ROWS = D_MODEL // LANES
DMA_UNROLL = 8
SCAN_UNROLL = 4
VMEM_LIMIT = 56 * 1024 * 1024

_C_QA, _C_KA, _C_VA, _C_GB, _C_XB, _C_QS, _C_KS, _C_VS, _C_GL, _C_END = (
    0, 512, 1024, 1536, 2048, 2560, 3072, 3328, 3584, 6656)


def _cparams(sem, vmem=VMEM_LIMIT):
    return pltpu.CompilerParams(dimension_semantics=sem, vmem_limit_bytes=vmem)


def _dot(a, b):
    return jnp.dot(a, b, preferred_element_type=F32)


def _dot_t(a, b):
    return lax.dot_general(a, b, (((1,), (1,)), ((), ())), preferred_element_type=F32)


def _rms(x, g):
    return x * lax.rsqrt(jnp.mean(x * x, axis=-1, keepdims=True) + RMS_EPS) * g


def _ada_kernel(c_ref, w_ref, b_ref, o_ref):
    c = c_ref[...]
    s = c * jax.nn.sigmoid(c)
    o_ref[...] = jnp.dot(s, w_ref[...], preferred_element_type=F32, precision=lax.Precision.HIGHEST) + b_ref[...]


def _ada(c8, w_ada, b_ada):
    depth, d, n = w_ada.shape
    tn = 1536
    return pl.pallas_call(
        _ada_kernel,
        grid=(depth, n // tn),
        in_specs=[pl.BlockSpec((8, d), lambda l, j: (0, 0)),
                  pl.BlockSpec((None, d, tn), lambda l, j: (l, 0, j)),
                  pl.BlockSpec((None, 1, tn), lambda l, j: (l, 0, j))],
        out_specs=pl.BlockSpec((None, 8, tn), lambda l, j: (l, 0, j)),
        out_shape=jax.ShapeDtypeStruct((depth, 8, n), F32),
        compiler_params=_cparams(("arbitrary", "arbitrary")),
        name="ada_mod",
    )(c8, w_ada, b_ada.reshape(depth, 1, n))


def _mod_row(mod_ref, b, t, k):
    row = jnp.where(t == 0, 2, b)
    return mod_ref[pl.ds(row, 1), k * D_MODEL:(k + 1) * D_MODEL]


def _inproj_kernel(x_ref, mod_ref, g_ref, w_ref, cos_ref, sin_ref,
                   qa_ref, ka_ref, va_ref, gb_ref, xb_ref, qs_ref, ks_ref, vs_ref, sg_ref):
    b, t = pl.program_id(0), pl.program_id(1)
    x = x_ref[...]
    h = _rms(x, g_ref[...]) * (1.0 + _mod_row(mod_ref, b, t, 1)) + _mod_row(mod_ref, b, t, 0)
    h = h.astype(BF16)

    def seg(lo, hi):
        return _dot(h, w_ref[:, lo:hi])

    qa_ref[...] = seg(_C_QA, _C_KA).astype(BF16)
    ka_ref[...] = seg(_C_KA, _C_VA).astype(BF16)
    va_ref[...] = seg(_C_VA, _C_GB).astype(BF16)
    gb_ref[...] = seg(_C_GB, _C_XB)
    xb_ref[...] = seg(_C_XB, _C_QS)
    cos = cos_ref[...]
    sin = sin_ref[...]
    half = ROPE_AXIS_DIM // 2
    lane = lax.broadcasted_iota(I32, (1, LANES), 1)
    first_half = (lane % ROPE_AXIS_DIM) < half

    def rope(v):
        partner = jnp.where(first_half, pltpu.roll(v, LANES - half, 1), pltpu.roll(v, half, 1))
        return (v * cos + partner * sin).astype(BF16)

    for j in range(4):
        qs_ref[:, j * LANES:(j + 1) * LANES] = rope(seg(_C_QS + j * LANES, _C_QS + (j + 1) * LANES))
    for j in range(2):
        ks_ref[:, j * LANES:(j + 1) * LANES] = rope(seg(_C_KS + j * LANES, _C_KS + (j + 1) * LANES))
    vs_ref[...] = seg(_C_VS, _C_GL).astype(BF16)
    for j in range(N_BRANCH):
        lo = _C_GL + j * D_MODEL
        sg_ref[:, j * D_MODEL:(j + 1) * D_MODEL] = jax.nn.sigmoid(seg(lo, lo + D_MODEL))


def _inproj(x_all, mod, g_pre, w_cat, cos_t, sin_t):
    bsz, s, d = x_all.shape
    nt = s // TILE
    tok = lambda b, t: (b, t, 0)
    const2 = lambda b, t: (0, 0)

    def out(width, dtype):
        return jax.ShapeDtypeStruct((bsz, s, width), dtype), pl.BlockSpec((None, TILE, width), tok)

    outs = [out(512, BF16), out(512, BF16), out(512, BF16), out(512, F32), out(512, F32),
            out(512, BF16), out(256, BF16), out(256, BF16), out(N_BRANCH * D_MODEL, F32)]
    return pl.pallas_call(
        _inproj_kernel,
        grid=(bsz, nt),
        in_specs=[pl.BlockSpec((None, TILE, d), tok),
                  pl.BlockSpec((8, 6 * d), const2),
                  pl.BlockSpec((1, d), const2),
                  pl.BlockSpec((d, _C_END), const2),
                  pl.BlockSpec((TILE, LANES), lambda b, t: (t, 0)),
                  pl.BlockSpec((TILE, LANES), lambda b, t: (t, 0))],
        out_specs=[o[1] for o in outs],
        out_shape=[o[0] for o in outs],
        compiler_params=_cparams(("arbitrary", "arbitrary")),
        name="inproj",
    )(x_all, mod, g_pre, w_cat, cos_t, sin_t)


def _two_head_attention(q, score_fn, value_fn):
    lane = lax.broadcasted_iota(I32, (1, LANES), 1)
    low = lane < HEAD_DIM
    outs = []
    for hh in range(2):
        qm = jnp.where(low if hh == 0 else jnp.logical_not(low), q, jnp.zeros_like(q))
        blocks = score_fn(qm, hh)
        m = blocks[0].max(axis=-1, keepdims=True)
        for s in blocks[1:]:
            m = jnp.maximum(m, s.max(axis=-1, keepdims=True))
        ps = [jnp.exp(s - m) for s in blocks]
        l = ps[0].sum(axis=-1, keepdims=True)
        for p in ps[1:]:
            l = l + p.sum(axis=-1, keepdims=True)
        o = value_fn([p.astype(BF16) for p in ps])
        outs.append(o / l)
    return jnp.where(low, outs[0], outs[1])


def _na_kernel(q_ref, k_ref, v_ref, tab_ref, o_ref, *, t_off, t_lat):
    t = pl.program_id(2) + t_off
    q = q_ref[...]
    kc = k_ref[0:CTX_LEN, :]
    vc = v_ref[0:CTX_LEN, :]

    if t_off == 0:
        @pl.when(t == 0)
        def _ctx():
            o = _two_head_attention(q, lambda qm, hh: [_dot_t(qm, kc)], lambda ps: _dot(ps[0], vc))
            o_ref[...] = o.astype(BF16)

    @pl.when(t > 0)
    def _lat():
        nk = NA_KROWS * GRID_W
        start = jnp.clip((t - 2) * TILE, 0, t_lat - nk)
        start = pl.multiple_of(start + CTX_LEN, TILE)
        k = k_ref[pl.ds(start, nk), :]
        v = v_ref[pl.ds(start, nk), :]
        o = _two_head_attention(
            q,
            lambda qm, hh: [_dot_t(qm, k) + tab_ref[hh], _dot_t(qm, kc)],
            lambda ps: _dot(ps[0], v) + _dot(ps[1], vc))
        o_ref[...] = o.astype(BF16)


def _na_attention(qa, ka, va, table, with_ctx):
    bsz, s, _ = qa.shape
    t_lat = s - CTX_LEN
    nrb = t_lat // TILE
    t_off = 0 if with_ctx else 1
    steps = nrb + 1 - t_off
    out_rows = steps * TILE

    def var(i):
        rb = jnp.maximum(i + t_off - 1, 0)
        return jnp.where(rb == 0, 0, jnp.where(rb == nrb - 1, 2, 1))

    return pl.pallas_call(
        functools.partial(_na_kernel, t_off=t_off, t_lat=t_lat),
        grid=(bsz, NA_HEADS // 2, steps),
        in_specs=[pl.BlockSpec((None, TILE, LANES), lambda b, p, i: (b, i + t_off, p)),
                  pl.BlockSpec((None, s, LANES), lambda b, p, i: (b, 0, p)),
                  pl.BlockSpec((None, s, LANES), lambda b, p, i: (b, 0, p)),
                  pl.BlockSpec((None, 2, TILE, NA_KROWS * GRID_W), lambda b, p, i: (var(i), p, 0, 0))],
        out_specs=pl.BlockSpec((None, TILE, LANES), lambda b, p, i: (b, i, p)),
        out_shape=jax.ShapeDtypeStruct((bsz, out_rows, BRANCH_W), BF16),
        compiler_params=_cparams(("arbitrary", "arbitrary", "arbitrary")),
        name="na_attention",
    )(qa, ka, va, table)


def _na_bias_table(rpb, rows):
    qr = np.arange(NA_QROWS)
    kr = np.arange(NA_KROWS)
    r0 = np.array([0, NA_WIN_R // 2, rows - NA_QROWS])
    kr0 = np.array([0, 0, rows - NA_KROWS])
    q_abs = r0[:, None] + qr[None, :]
    k_abs = kr0[:, None] + kr[None, :]
    rs = np.clip(q_abs - NA_WIN_R // 2, 0, rows - NA_WIN_R)
    rvalid = (k_abs[:, None, :] >= rs[:, :, None]) & (k_abs[:, None, :] < rs[:, :, None] + NA_WIN_R)
    ridx = np.clip(k_abs[:, None, :] - q_abs[:, :, None] + NA_WIN_R - 1, 0, 2 * NA_WIN_R - 2)
    qc = np.arange(GRID_W)
    kc = np.arange(GRID_W)
    q_start = np.clip(qc - NA_WIN_C // 2, 0, GRID_W - NA_WIN_C)
    cvalid = (kc[None, :] >= q_start[:, None]) & (kc[None, :] < q_start[:, None] + NA_WIN_C)
    cidx = np.clip(kc[None, :] - qc[:, None] + NA_WIN_C - 1, 0, 2 * NA_WIN_C - 2)
    nr, ncol = 2 * NA_WIN_R - 1, 2 * NA_WIN_C - 1
    rsel = ((ridx[..., None] == np.arange(nr)) & rvalid[..., None]).astype(np.float32)
    csel = ((cidx[..., None] == np.arange(ncol)) & cvalid[..., None]).astype(np.float32)
    hp = lax.Precision.HIGHEST
    part = jnp.einsum("hij,vqki->hvqkj", rpb.astype(F32), jnp.asarray(rsel), precision=hp)
    tab = jnp.einsum("hvqkj,cdj->vhqckd", part, jnp.asarray(csel), precision=hp)
    rmask = jnp.asarray(np.where(rvalid, 0.0, NEG_INF).astype(np.float32))
    cmask = jnp.asarray(np.where(cvalid, 0.0, NEG_INF).astype(np.float32))
    tab = tab + rmask[:, None, :, None, :, None] + cmask[None, None, None, :, None, :]
    return tab.reshape(3, NA_HEADS, NA_QROWS * GRID_W, NA_KROWS * GRID_W)


def _sw_kernel(q_ref, k_ref, v_ref, mask_ref, sink_ref, o_ref, *, t_off, t_lat):
    t = pl.program_id(2) + t_off
    q = q_ref[...]
    lane = lax.broadcasted_iota(I32, (1, LANES), 1)
    low = lane < HEAD_DIM
    zero = jnp.zeros((SW_BLOCK, LANES), BF16)
    q01, q23 = q[:, :LANES], q[:, LANES:]
    qs = jnp.concatenate([jnp.where(low, q01, zero), jnp.where(low, zero, q01),
                          jnp.where(low, q23, zero), jnp.where(low, zero, q23)], axis=0)
    kc = k_ref[0:CTX_LEN, :]
    vc = v_ref[0:CTX_LEN, :]
    sink = sink_ref[...]
    s_cx = _dot_t(qs, kc)

    def finish(blocks, vals):
        m = sink
        for s in blocks:
            m = jnp.maximum(m, s.max(axis=-1, keepdims=True))
        ps = [jnp.exp(s - m) for s in blocks]
        l = jnp.exp(sink - m)
        for p in ps:
            l = l + p.sum(axis=-1, keepdims=True)
        o = _dot(ps[0].astype(BF16), vals[0])
        for p, v in zip(ps[1:], vals[1:]):
            o = o + _dot(p.astype(BF16), v)
        o = o / l
        b = SW_BLOCK
        o_ref[:, :LANES] = jnp.where(low, o[0:b], o[b:2 * b]).astype(BF16)
        o_ref[:, LANES:] = jnp.where(low, o[2 * b:3 * b], o[3 * b:4 * b]).astype(BF16)

    n_ctx_tiles = CTX_LEN // SW_BLOCK
    if t_off == 0:
        @pl.when(t < n_ctx_tiles)
        def _ctx():
            finish([s_cx], [vc])

    @pl.when(t >= n_ctx_tiles)
    def _lat():
        nk = 3 * SW_BLOCK
        start = jnp.clip((t - n_ctx_tiles - 1) * SW_BLOCK, 0, t_lat - nk)
        start = pl.multiple_of(start + CTX_LEN, SW_BLOCK)
        k = k_ref[pl.ds(start, nk), :]
        v = v_ref[pl.ds(start, nk), :]
        finish([_dot_t(qs, k) + mask_ref[...], s_cx], [v, vc])


def _sw_attention(qs, ks, vs, mask, sink_col, with_ctx):
    bsz, s, _ = qs.shape
    t_lat = s - CTX_LEN
    nb = t_lat // SW_BLOCK
    n_ctx_tiles = CTX_LEN // SW_BLOCK
    t_off = 0 if with_ctx else n_ctx_tiles
    steps = nb + n_ctx_tiles - t_off

    def var(i):
        n = jnp.maximum(i + t_off - n_ctx_tiles, 0)
        return jnp.where(n == 0, 0, jnp.where(n == nb - 1, 2, 1))

    g = SW_HEADS // SW_KV_HEADS
    return pl.pallas_call(
        functools.partial(_sw_kernel, t_off=t_off, t_lat=t_lat),
        grid=(bsz, SW_KV_HEADS, steps),
        in_specs=[pl.BlockSpec((None, SW_BLOCK, 2 * LANES), lambda b, h, i: (b, i + t_off, h)),
                  pl.BlockSpec((None, s, LANES), lambda b, h, i: (b, 0, h)),
                  pl.BlockSpec((None, s, LANES), lambda b, h, i: (b, 0, h)),
                  pl.BlockSpec((None, g * SW_BLOCK, 3 * SW_BLOCK), lambda b, h, i: (var(i), 0, 0)),
                  pl.BlockSpec((None, g * SW_BLOCK, 1), lambda b, h, i: (h, 0, 0))],
        out_specs=pl.BlockSpec((None, SW_BLOCK, 2 * LANES), lambda b, h, i: (b, i, h)),
        out_shape=jax.ShapeDtypeStruct((bsz, steps * SW_BLOCK, BRANCH_W), BF16),
        compiler_params=_cparams(("arbitrary", "arbitrary", "arbitrary")),
        name="sw_attention",
    )(qs, ks, vs, mask, sink_col)


def _sw_mask_table():
    i = np.arange(SW_BLOCK)[:, None]
    j = np.arange(3 * SW_BLOCK)[None, :]
    tabs = []
    for shift in (0, SW_BLOCK, 2 * SW_BLOCK):
        rel = j - i - shift
        tabs.append(np.where(np.abs(rel) <= SW_WINDOW, 0.0, NEG_INF).astype(np.float32))
    tab = np.stack(tabs)
    return jnp.asarray(np.tile(tab, (1, SW_HEADS // SW_KV_HEADS, 1)))


def _lru_kernel(x_ref, xp_ref, xn_ref, cw_ref, cb_ref, wg_ref, bg_ref, c8_ref, o_ref,
                xs, a_s, b_s, carry, *, nc):
    d, i = pl.program_id(1), pl.program_id(2)
    ci = jnp.where(d == 0, i, jnp.where(i == 0, 0, nc - i))
    tc = TILE
    seg = tc // 8
    w = LRU_WIDTH

    no_prev = (ci == 0) | (ci == 1)
    no_next = (ci == 0) | (ci == nc - 1)
    xs[0:8, :] = jnp.where(no_prev, 0.0, xp_ref[...])
    xs[8:8 + tc, :] = x_ref[...]
    xs[8 + tc:16 + tc, :] = jnp.where(no_next, 0.0, xn_ref[...])
    u = cb_ref[...] + cw_ref[0:1, :] * xs[7:7 + tc, :]
    for j in range(1, LRU_CONV):
        u = u + cw_ref[j:j + 1, :] * xs[7 + j:7 + j + tc, :]

    g = _dot(u.astype(BF16), wg_ref[...]) + bg_ref[...]
    r = jax.nn.sigmoid(g[:, :w])
    ig = jax.nn.sigmoid(g[:, w:])
    log_a = c8_ref[...] * r
    a = jnp.exp(log_a)
    th = jnp.tanh(log_a)
    bb = jnp.sqrt(-2.0 * th / (1.0 - th)) * (ig * u)
    nl = w // LANES
    for cidx in range(nl):
        a_s[cidx] = a[:, cidx * LANES:(cidx + 1) * LANES]
        b_s[cidx] = bb[:, cidx * LANES:(cidx + 1) * LANES]

    @pl.when(i == 0)
    def _init():
        carry[...] = jnp.zeros_like(carry)

    def body(jj, hp):
        hs, ps = hp
        j = jnp.where(d == 0, jj, seg - 1 - jj)
        hs_new, ps_new = [], []
        for cidx in range(nl):
            at = a_s[cidx, pl.ds(j, 8, stride=seg), :]
            bt = b_s[cidx, pl.ds(j, 8, stride=seg), :]
            h = at * hs[cidx] + bt
            p = at * ps[cidx]
            b_s[cidx, pl.ds(j, 8, stride=seg), :] = h
            a_s[cidx, pl.ds(j, 8, stride=seg), :] = p
            hs_new.append(h)
            ps_new.append(p)
        return tuple(hs_new), tuple(ps_new)

    init = (tuple(jnp.zeros((8, LANES), F32) for _ in range(nl)), tuple(jnp.ones((8, LANES), F32) for _ in range(nl)))
    hs_end, ps_end = lax.fori_loop(0, seg, body, init, unroll=SCAN_UNROLL)
    h_end = jnp.concatenate(hs_end, axis=1)
    p_end = jnp.concatenate(ps_end, axis=1)

    c_in = carry[0:1, :]

    def fix(sidx, c):
        rows = slice(sidx * seg, (sidx + 1) * seg)
        for cidx in range(nl):
            lanes = slice(cidx * LANES, (cidx + 1) * LANES)
            o_ref[rows, lanes] = b_s[cidx, rows, :] + a_s[cidx, rows, :] * c[:, lanes]
        return h_end[sidx:sidx + 1, :] + p_end[sidx:sidx + 1, :] * c

    @pl.when(d == 0)
    def _fwd():
        c = c_in
        for sidx in range(8):
            c = fix(sidx, c)
        carry[0:1, :] = c

    @pl.when(d == 1)
    def _bwd():
        c = c_in
        for sidx in range(7, -1, -1):
            c = fix(sidx, c)
        carry[0:1, :] = c


def _lru(xb, conv_w, conv_b, wg, bg, c8):
    bsz, s, w = xb.shape
    nc = s // TILE
    r8 = TILE // 8

    def chunk(d, i):
        return jnp.where(d == 0, i, jnp.where(i == 0, 0, nc - i))

    return pl.pallas_call(
        functools.partial(_lru_kernel, nc=nc),
        grid=(bsz, 2, nc),
        in_specs=[pl.BlockSpec((None, TILE, w), lambda b, d, i: (b, chunk(d, i), 0)),
                  pl.BlockSpec((None, 8, w), lambda b, d, i: (b, jnp.maximum(chunk(d, i) * r8 - 1, 0), 0)),
                  pl.BlockSpec((None, 8, w), lambda b, d, i: (b, jnp.minimum((chunk(d, i) + 1) * r8, s // 8 - 1), 0)),
                  pl.BlockSpec((LRU_CONV, w), lambda b, d, i: (0, 0)),
                  pl.BlockSpec((1, w), lambda b, d, i: (0, 0)),
                  pl.BlockSpec((None, w, 2 * w), lambda b, d, i: (d, 0, 0)),
                  pl.BlockSpec((None, 1, 2 * w), lambda b, d, i: (d, 0, 0)),
                  pl.BlockSpec((None, 1, w), lambda b, d, i: (d, 0, 0))],
        out_specs=pl.BlockSpec((None, None, TILE, w), lambda b, d, i: (d, b, chunk(d, i), 0)),
        out_shape=jax.ShapeDtypeStruct((2, bsz, s, w), F32),
        scratch_shapes=[pltpu.VMEM((TILE + 16, w), F32), pltpu.VMEM((w // LANES, TILE, LANES), F32),
                        pltpu.VMEM((w // LANES, TILE, LANES), F32), pltpu.VMEM((8, w), F32)],
        compiler_params=_cparams(("arbitrary", "arbitrary", "arbitrary")),
        name="rglru",
    )(xb, xb, xb, conv_w, conv_b, wg, bg, c8)


def _gelu_tanh(x):
    return 0.5 * x * (1.0 + jnp.tanh(np.sqrt(2.0 / np.pi).astype(np.float32) * (x + 0.044715 * (x * x * x))))


def _merge_kernel(x_ref, ya_ref, hf_ref, hb_ref, gb_ref, yc_ref, sg_ref, mod_ref, gpost_ref, gpre_ref,
                  wbr_ref, wout_ref, wr_ref, br_ref, cnt0_ref,
                  x1_ref, h2_ref, idx_ref, wgt_ref, rank_ref, cnt_ref, cnt_s, *, t_off):
    b, ti = pl.program_id(0), pl.program_id(1)
    t = ti + t_off
    d = D_MODEL

    @pl.when((b == 0) & (ti == 0))
    def _init():
        cnt_s[...] = cnt0_ref[...]

    yb = (_gelu_tanh(gb_ref[...]) * (hf_ref[...] + hb_ref[...])).astype(BF16)
    ys = (ya_ref[...], yb, yc_ref[...])
    mix = sg_ref[:, 0:d] * _dot(ys[0], wbr_ref[0])
    for k in range(1, N_BRANCH):
        mix = mix + sg_ref[:, k * d:(k + 1) * d] * _dot(ys[k], wbr_ref[k])
    y = _dot(mix.astype(BF16), wout_ref[...])
    x1 = x_ref[...] + _mod_row(mod_ref, b, t, 2) * _rms(y, gpost_ref[...])
    x1_ref[...] = x1
    h2 = _rms(x1, gpre_ref[...]) * (1.0 + _mod_row(mod_ref, b, t, 4)) + _mod_row(mod_ref, b, t, 3)
    _store_token_rows(h2_ref, h2, TILE)

    logits = _dot(h2.astype(BF16), wr_ref[...]) + br_ref[...]
    lane = lax.broadcasted_iota(I32, (TILE, LANES), 1)
    work = logits
    tops, sels = [], []
    for k in range(TOP_K):
        m = work.max(axis=-1, keepdims=True)
        first = jnp.where(work == m, lane, LANES).min(axis=-1, keepdims=True)
        sel = lane == first
        work = jnp.where(sel, -3.0e38, work)
        tops.append(m)
        sels.append(sel)
    es = [jnp.exp(m - tops[0]) for m in tops]
    den = es[0] + es[1] + es[2] + es[3]

    chosen = jnp.zeros((TILE, LANES), F32)
    for sel in sels:
        chosen = jnp.where(sel, 1.0, chosen)
    rr = lax.broadcasted_iota(I32, (TILE, TILE), 0)
    cc = lax.broadcasted_iota(I32, (TILE, TILE), 1)
    tri = jnp.where(cc < rr, 1.0, 0.0).astype(BF16)
    pref = _dot(tri, chosen.astype(BF16)) + cnt_s[0:1, :]
    cnt_new = cnt_s[0:1, :] + chosen.sum(axis=0, keepdims=True)
    cnt_s[0:1, :] = cnt_new
    cnt_ref[...] = jnp.broadcast_to(cnt_new, cnt_ref.shape)

    idx_o = jnp.zeros((TILE, LANES), I32)
    wgt_o = jnp.zeros((TILE, LANES), F32)
    rank_o = jnp.zeros((TILE, LANES), I32)
    for k in range(TOP_K):
        first = jnp.where(sels[k], lane, 0).max(axis=-1, keepdims=True)
        rk = jnp.where(sels[k], pref, 0.0).sum(axis=-1, keepdims=True)
        idx_o = jnp.where(lane == k, first, idx_o)
        wgt_o = jnp.where(lane == k, es[k] / den, wgt_o)
        rank_o = jnp.where(lane == k, rk.astype(I32), rank_o)
    idx_ref[...] = idx_o
    wgt_ref[...] = wgt_o
    rank_ref[...] = rank_o


def _merge(x_all, ya, hfb, gb, yc, sg, mod, g_post, g_pre, w_br, w_out, w_r, b_r, cnt0, with_ctx):
    bsz, s, d = x_all.shape
    t_off = 0 if with_ctx else 1
    nt = s // TILE - t_off
    rows = nt * TILE
    inp = lambda b, t: (b, t + t_off, 0)
    loc = lambda b, t: (b, t, 0)
    c2 = lambda b, t: (0, 0)
    c3 = lambda b, t: (0, 0, 0)

    def out(width, dtype):
        return jax.ShapeDtypeStruct((bsz, rows, width), dtype), pl.BlockSpec((None, TILE, width), loc)

    h2_out = (jax.ShapeDtypeStruct((bsz, rows * TOK_ROWS, LANES), F32),
              pl.BlockSpec((None, TILE * TOK_ROWS, LANES), loc))
    outs = [out(d, F32), h2_out, out(LANES, I32), out(LANES, F32), out(LANES, I32)]
    shapes = [o[0] for o in outs] + [jax.ShapeDtypeStruct((8, LANES), F32)]
    specs = [o[1] for o in outs] + [pl.BlockSpec((8, LANES), c2)]
    return pl.pallas_call(
        functools.partial(_merge_kernel, t_off=t_off),
        grid=(bsz, nt),
        in_specs=[pl.BlockSpec((None, TILE, d), inp),
                  pl.BlockSpec((None, TILE, BRANCH_W), loc),
                  pl.BlockSpec((None, None, TILE, BRANCH_W), lambda b, t: (0, b, t + t_off, 0)),
                  pl.BlockSpec((None, None, TILE, BRANCH_W), lambda b, t: (1, b, t + t_off, 0)),
                  pl.BlockSpec((None, TILE, BRANCH_W), inp),
                  pl.BlockSpec((None, TILE, BRANCH_W), loc),
                  pl.BlockSpec((None, TILE, N_BRANCH * d), inp),
                  pl.BlockSpec((8, 6 * d), c2),
                  pl.BlockSpec((1, d), c2),
                  pl.BlockSpec((1, d), c2),
                  pl.BlockSpec((N_BRANCH, BRANCH_W, d), c3),
                  pl.BlockSpec((d, d), c2),
                  pl.BlockSpec((d, LANES), c2),
                  pl.BlockSpec((1, LANES), c2),
                  pl.BlockSpec((8, LANES), c2)],
        out_specs=specs,
        out_shape=shapes,
        scratch_shapes=[pltpu.VMEM((8, LANES), F32)],
        compiler_params=_cparams(("arbitrary", "arbitrary")),
        name="merge_router",
    )(x_all, ya, hfb, hfb, gb, yc, sg, mod, g_post, g_pre, w_br, w_out, w_r, b_r, cnt0)


def _store_token_rows(ref, val, n, lead=()):
    for j in range(TOK_ROWS):
        ref[lead + (pl.ds(j, n, stride=TOK_ROWS), slice(None))] = val[:, j * LANES:(j + 1) * LANES]


def _load_token_rows(ref, n, lead=()):
    return jnp.concatenate([ref[lead + (pl.ds(j, n, stride=TOK_ROWS), slice(None))] for j in range(TOK_ROWS)], axis=1)


def _token_slice(i):
    return pl.ds(pl.multiple_of(i * TOK_ROWS, TOK_ROWS), TOK_ROWS)


def _dispatch_kernel(pos_ref, last_ref, h_ref, xs_ref, zbuf, sem, zsem):
    @pl.when((pl.program_id(0) == 0) & (pl.program_id(1) == 0))
    def _zero_fill():
        zbuf[...] = jnp.zeros_like(zbuf)

        def fill(e):
            dst = xs_ref.at[pl.ds(pl.multiple_of(last_ref[0, e] * TOK_ROWS, TOK_ROWS), MOE_ROWS * TOK_ROWS)]
            return pltpu.make_async_copy(zbuf, dst, zsem)

        for e in range(N_EXPERTS):
            @pl.when(last_ref[0, e] >= 0)
            def _():
                fill(e).start()
        for e in range(N_EXPERTS):
            @pl.when(last_ref[0, e] >= 0)
            def _():
                fill(e).wait()

    def row_copy(i, k):
        p = pos_ref[0, 0, i * TOP_K + k]
        return pltpu.make_async_copy(h_ref.at[_token_slice(i)], xs_ref.at[_token_slice(p)], sem)

    def start(i, c):
        for k in range(TOP_K):
            row_copy(i, k).start(priority=k % 2)
        return c

    lax.fori_loop(0, TILE, start, 0, unroll=DMA_UNROLL)
    for _ in range(TOP_K):
        pltpu.make_async_copy(h_ref, xs_ref.at[pl.ds(0, TILE * TOK_ROWS)], sem).wait()


def _dispatch(h2, pos, last_tile_row, n_rows):
    bsz, rows = pos.shape[:2]
    nt = rows // TILE
    return pl.pallas_call(
        _dispatch_kernel,
        grid=(bsz, nt),
        in_specs=[pl.BlockSpec((1, 1, TILE * TOP_K), lambda b, t: (b * nt + t, 0, 0), memory_space=pltpu.SMEM),
                  pl.BlockSpec((1, N_EXPERTS), lambda b, t: (0, 0), memory_space=pltpu.SMEM),
                  pl.BlockSpec((TILE * TOK_ROWS, LANES), lambda b, t: (b * nt + t, 0))],
        out_specs=pl.BlockSpec(memory_space=pl.ANY),
        out_shape=jax.ShapeDtypeStruct((n_rows * TOK_ROWS, LANES), F32),
        scratch_shapes=[pltpu.VMEM((MOE_ROWS * TOK_ROWS, LANES), F32), pltpu.SemaphoreType.DMA(()),
                        pltpu.SemaphoreType.DMA(())],
        compiler_params=_cparams(("arbitrary", "arbitrary")),
        name="moe_dispatch",
    )(pos.reshape(bsz * nt, 1, TILE * TOP_K), last_tile_row, h2.reshape(bsz * rows * TOK_ROWS, LANES))


def _expert_kernel(te_ref, nu_ref, x_ref, w1_ref, b1_ref, w2_ref, b2_ref, y_ref, w1_s, w2_s):
    t = pl.program_id(0)
    prev = te_ref[jnp.maximum(t - 1, 0)]
    active = t < nu_ref[0]

    @pl.when(active & ((t == 0) | (te_ref[t] != prev)))
    def _load():
        w1_s[...] = w1_ref[...].astype(BF16)
        w2_s[...] = w2_ref[...].astype(BF16)

    @pl.when(active)
    def _run():
        f = D_EXPERT
        x = _load_token_rows(x_ref, MOE_ROWS).astype(BF16)
        hid = _dot(x, w1_s[...]) + b1_ref[...]
        glu = jnp.minimum(hid[:, :f], SWIGLU_LIMIT)
        lin = jnp.clip(hid[:, f:], -SWIGLU_LIMIT, SWIGLU_LIMIT)
        act = glu * jax.nn.sigmoid(SWIGLU_ALPHA * glu) * (lin + 1.0)
        _store_token_rows(y_ref, _dot(act.astype(BF16), w2_s[...]) + b2_ref[...], MOE_ROWS)

    @pl.when(jnp.logical_not(active))
    def _idle():
        y_ref[...] = jnp.zeros_like(y_ref)


def _experts(xs, tile_expert, n_used, w1, b1, w2, b2, layer):
    depth, e, d, f2 = w1.shape
    n_tiles = xs.shape[0] // (MOE_ROWS * TOK_ROWS)
    blk = (MOE_ROWS * TOK_ROWS, LANES)
    xmap = lambda t, te, nu: (jnp.minimum(t, nu[0] - 1), 0)
    wmap = lambda t, te, nu: (layer, te[t], 0, 0)
    grid_spec = pltpu.PrefetchScalarGridSpec(
        num_scalar_prefetch=2,
        grid=(n_tiles,),
        in_specs=[pl.BlockSpec(blk, xmap),
                  pl.BlockSpec((None, None, d, f2), wmap),
                  pl.BlockSpec((None, None, 1, f2), wmap),
                  pl.BlockSpec((None, None, f2 // 2, d), wmap),
                  pl.BlockSpec((None, None, 1, d), wmap)],
        out_specs=pl.BlockSpec(blk, lambda t, te, nu: (t, 0)),
        scratch_shapes=[pltpu.VMEM((d, f2), BF16), pltpu.VMEM((f2 // 2, d), BF16)])
    return pl.pallas_call(
        _expert_kernel,
        grid_spec=grid_spec,
        out_shape=jax.ShapeDtypeStruct(xs.shape, F32),
        compiler_params=_cparams(("arbitrary",)),
        name="moe_experts",
    )(tile_expert, n_used, xs, w1, b1.reshape(depth, e, 1, f2), w2, b2.reshape(depth, e, 1, d))


def _combine_kernel(pos_ref, ys_ref, wgt_ref, x1_ref, mod_ref, gpost_ref, o_ref, buf, sem, *, t_off):
    b, t = pl.program_id(0), pl.program_id(1) + t_off

    def row_copy(i, k):
        p = pos_ref[0, 0, i * TOP_K + k]
        return pltpu.make_async_copy(ys_ref.at[_token_slice(p)], buf.at[k, _token_slice(i)], sem)

    def start(i, c):
        for k in range(TOP_K):
            row_copy(i, k).start(priority=k % 2)
        return c

    lax.fori_loop(0, TILE, start, 0, unroll=DMA_UNROLL)
    for k in range(TOP_K):
        pltpu.make_async_copy(ys_ref.at[pl.ds(0, TILE * TOK_ROWS)], buf.at[k], sem).wait()

    wgt = wgt_ref[...]
    moe = wgt[:, 0:1] * _load_token_rows(buf, TILE, (0,))
    for k in range(1, TOP_K):
        moe = moe + wgt[:, k:k + 1] * _load_token_rows(buf, TILE, (k,))
    o_ref[...] = x1_ref[...] + _mod_row(mod_ref, b, t, 5) * _rms(moe, gpost_ref[...])


def _combine(ys, pos, wgt, x1, mod, g_post, with_ctx):
    bsz, rows, d = x1.shape
    nt = rows // TILE
    tok = lambda b, t: (b, t, 0)
    return pl.pallas_call(
        functools.partial(_combine_kernel, t_off=0 if with_ctx else 1),
        grid=(bsz, nt),
        in_specs=[pl.BlockSpec((1, 1, TILE * TOP_K), lambda b, t: (b * nt + t, 0, 0), memory_space=pltpu.SMEM),
                  pl.BlockSpec(memory_space=pl.ANY),
                  pl.BlockSpec((None, TILE, LANES), tok),
                  pl.BlockSpec((None, TILE, d), tok),
                  pl.BlockSpec((8, 6 * d), lambda b, t: (0, 0)),
                  pl.BlockSpec((1, d), lambda b, t: (0, 0))],
        out_specs=pl.BlockSpec((None, TILE, d), tok),
        out_shape=jax.ShapeDtypeStruct((bsz, rows, d), F32),
        scratch_shapes=[pltpu.VMEM((TOP_K, TILE * TOK_ROWS, LANES), F32), pltpu.SemaphoreType.DMA(())],
        compiler_params=_cparams(("arbitrary", "arbitrary")),
        name="moe_combine",
    )(pos.reshape(bsz * nt, 1, TILE * TOP_K), ys, wgt, x1, mod, g_post)


def _moe(h2, idx, wgt, rank, counts, x1, mod, g_post, w1, b1, w2, b2, layer, with_ctx):
    bsz, rows, d = x1.shape
    n_tiles = (bsz * rows * TOP_K) // MOE_ROWS + N_EXPERTS
    n_rows = n_tiles * MOE_ROWS
    cnt = counts.astype(I32)
    tiles_e = (cnt + MOE_ROWS - 1) // MOE_ROWS
    tile_end = jnp.cumsum(tiles_e)
    row_start = (tile_end - tiles_e) * MOE_ROWS
    n_used = tile_end[-1:]
    tile_ids = jnp.minimum(jnp.arange(n_tiles, dtype=I32), n_used[0] - 1)
    tile_expert = jnp.sum((tile_end[None, :] <= tile_ids[:, None]).astype(I32), axis=1)
    tile_expert = jnp.minimum(tile_expert, N_EXPERTS - 1)
    experts = jnp.arange(N_EXPERTS, dtype=I32)
    start_of = jnp.sum(jnp.where(idx[..., :TOP_K, None] == experts, row_start, 0), axis=-1)
    pos = (start_of + rank[..., :TOP_K]).astype(I32)
    last_tile_row = jnp.where(tiles_e > 0, (tile_end - 1) * MOE_ROWS, -1).astype(I32)[None, :]
    xs = _dispatch(h2, pos, last_tile_row, n_rows)
    ys = _experts(xs, tile_expert, n_used.astype(I32), w1, b1, w2, b2, layer)
    return _combine(ys, pos, wgt, x1, mod, g_post, with_ctx)


def _dup_heads(w):
    d = w.shape[0]
    w = w.reshape(d, -1, 1, HEAD_DIM)
    return jnp.broadcast_to(w, (d, w.shape[1], 2, HEAD_DIM)).reshape(d, -1)


def _inproj_weight(w_in):
    sizes = (BRANCH_W, BRANCH_W, BRANCH_W, LRU_WIDTH, LRU_WIDTH, BRANCH_W,
             SW_KV_HEADS * HEAD_DIM, SW_KV_HEADS * HEAD_DIM, N_BRANCH * D_MODEL)
    offs = np.cumsum((0,) + sizes)
    qa, ka, va, gb, xb, qs, ks, vs, gl = [w_in[:, offs[i]:offs[i + 1]] for i in range(9)]
    scale = HEAD_DIM ** -0.5
    cols = [qa * scale, ka, va, gb, xb, qs * scale, _dup_heads(ks), _dup_heads(vs), gl]
    return jnp.concatenate(cols, axis=1).astype(BF16)


def _rope_tables(t_lat):
    pos = np.arange(t_lat)
    row = (pos // GRID_W).astype(np.float32)
    col = (pos % GRID_W).astype(np.float32)
    half = ROPE_AXIS_DIM // 2
    inv = (ROPE_BASE ** (-jnp.arange(0, ROPE_AXIS_DIM, 2, dtype=F32) / ROPE_AXIS_DIM))
    d = np.arange(HEAD_DIM)
    axis = d // ROPE_AXIS_DIM
    freq = d % half
    sign = np.where((d % ROPE_AXIS_DIM) < half, -1.0, 1.0).astype(np.float32)
    p = jnp.where(jnp.asarray(axis == 0)[None, :], jnp.asarray(row)[:, None], jnp.asarray(col)[:, None])
    ang = p * inv[freq][None, :]
    cos = jnp.concatenate([jnp.ones((CTX_LEN, HEAD_DIM), F32), jnp.cos(ang)], axis=0)
    sin = jnp.concatenate([jnp.zeros((CTX_LEN, HEAD_DIM), F32), jnp.sin(ang) * sign[None, :]], axis=0)
    return jnp.tile(cos, (1, 2)), jnp.tile(sin, (1, 2))


def _lru_gate_weights(wa, ba, wx, bx):
    def dense(wblk):
        k, bs = wblk.shape[1], wblk.shape[2]
        eye = jnp.eye(k, dtype=wblk.dtype)
        return jnp.einsum("dkij,kl->dkilj", wblk, eye).reshape(2, k * bs, k * bs)
    wg = jnp.concatenate([dense(wa), dense(wx)], axis=-1).astype(BF16)
    bg = jnp.concatenate([ba, bx], axis=-1)[:, None, :].astype(F32)
    return wg, bg


def _layer(x_all, mod, g_mix_pre, g_mix_post, g_ffn_pre, g_ffn_post, w_in, na_rpb, conv_w, conv_b,
           lru_wa, lru_ba, lru_wx, lru_bx, lru_lam, sw_sinks, w_branch, w_out, w_router, b_router,
           w1, b1, w2, b2, layer, with_ctx):
    bsz, s, d = x_all.shape
    t_lat = s - CTX_LEN
    row = lambda v: v.reshape(1, -1).astype(F32)

    cos_t, sin_t = _rope_tables(t_lat)
    qa, ka, va, gb, xb, qs, ks, vs, sg = _inproj(x_all, mod, row(g_mix_pre), _inproj_weight(w_in), cos_t, sin_t)

    ya = _na_attention(qa, ka, va, _na_bias_table(na_rpb, t_lat // GRID_W), with_ctx)

    wg, bg = _lru_gate_weights(lru_wa, lru_ba, lru_wx, lru_bx)
    c8 = (-LRU_C * jax.nn.softplus(-lru_lam.astype(F32)))[:, None, :]
    hfb = _lru(xb, conv_w.astype(F32), row(conv_b), wg, bg, c8)

    g = SW_HEADS // SW_KV_HEADS
    sink_col = jnp.repeat(sw_sinks.astype(F32).reshape(SW_KV_HEADS, g), SW_BLOCK, axis=1)[:, :, None]
    yc = _sw_attention(qs, ks, vs, _sw_mask_table(), sink_col, with_ctx)

    w_r = jnp.zeros((d, LANES), F32).at[:, :N_EXPERTS].set(w_router).astype(BF16)
    b_r = jnp.full((1, LANES), NEG_INF, F32).at[0, :N_EXPERTS].set(b_router.astype(F32))
    x1, h2, idx, wgt, rank, cnt = _merge(
        x_all, ya, hfb, gb, yc, sg, mod, row(g_mix_post), row(g_ffn_pre), w_branch.astype(BF16),
        w_out.astype(BF16), w_r, b_r, jnp.zeros((8, LANES), F32), with_ctx)

    return _moe(h2, idx, wgt, rank, cnt[0, :N_EXPERTS], x1, mod, row(g_ffn_post), w1, b1, w2, b2, layer, with_ctx)


def kernel(x, c, ctx, c_ctx, w_ada, b_ada, g_mix_pre, g_mix_post, g_ffn_pre, g_ffn_post, w_in, na_rpb, conv_w, conv_b, lru_wa, lru_ba, lru_wx, lru_bx, lru_lam, sw_sinks, w_branch, w_out, w_router, b_router, w1, b1, w2, b2):
    depth = w_ada.shape[0]
    bsz, t_lat, d = x.shape
    assert bsz == 2 and d == D_MODEL and ctx.shape[1] == CTX_LEN
    assert t_lat % TILE == 0 and t_lat // GRID_W >= NA_KROWS and t_lat // SW_BLOCK >= 3
    c8 = jnp.zeros((8, d), F32).at[:bsz].set(c).at[bsz].set(c_ctx)
    mods = _ada(c8, w_ada, b_ada)
    x_all = jnp.concatenate([ctx, x], axis=1)
    for l in range(depth):
        with_ctx = l < depth - 1
        out = _layer(x_all, mods[l], g_mix_pre[l], g_mix_post[l], g_ffn_pre[l], g_ffn_post[l], w_in[l], na_rpb[l],
                     conv_w[l], conv_b[l], lru_wa[l], lru_ba[l], lru_wx[l], lru_bx[l], lru_lam[l], sw_sinks[l],
                     w_branch[l], w_out[l], w_router[l], b_router[l], w1, b1, w2, b2, l, with_ctx)
        x_all = out
    return x_all
```

```python
import functools

import numpy as np
import jax
import jax.numpy as jnp
from jax import lax
from jax.experimental import pallas as pl
from jax.experimental.pallas import tpu as pltpu

F32 = jnp.float32
BF16 = jnp.bfloat16
I32 = jnp.int32

D_MODEL = 1024
CTX_LEN = 256
GRID_W = 64
HEAD_DIM = 64
BRANCH_W = 512
N_BRANCH = 3
NA_HEADS = 8
NA_WIN_R = 8
NA_WIN_C = 16
LRU_WIDTH = 512
LRU_BLOCKS = 8
LRU_CONV = 4
LRU_C = 8.0
SW_HEADS = 8
SW_KV_HEADS = 2
SW_WINDOW = 128
SW_BLOCK = 128
ROPE_BASE = 10000.0
ROPE_AXIS_DIM = HEAD_DIM // 2
N_EXPERTS = 32
TOP_K = 4
D_EXPERT = 1024
SWIGLU_LIMIT = 7.0
SWIGLU_ALPHA = 1.702
RMS_EPS = 1e-6
NEG_INF = -1e30

LANES = 128
TILE = 256
NA_QROWS = 4
NA_KROWS = 12
NA_PAIRS = 2
MOE_ROWS = 512
TOK_ROWS = D_MODEL // LANES
DMA_UNROLL = 8
VMEM_LIMIT = 56 * 1024 * 1024

_C_QA, _C_KA, _C_VA, _C_GB, _C_XB, _C_QS, _C_KS, _C_VS, _C_GL, _C_END = (
    0, 512, 1024, 1536, 2048, 2560, 3072, 3328, 3584, 6656)


def _cparams(sem, vmem=VMEM_LIMIT):
    return pltpu.CompilerParams(dimension_semantics=sem, vmem_limit_bytes=vmem)


def _dot(a, b):
    return jnp.dot(a, b, preferred_element_type=F32)


def _dot_t(a, b):
    return lax.dot_general(a, b, (((1,), (1,)), ((), ())), preferred_element_type=F32)


def _rms(x, g):
    return x * lax.rsqrt(jnp.mean(x * x, axis=-1, keepdims=True) + RMS_EPS) * g


def _ada_kernel(c_ref, w_ref, b_ref, o_ref):
    c = c_ref[...]
    s = c * jax.nn.sigmoid(c)
    o_ref[...] = jnp.dot(s, w_ref[...], preferred_element_type=F32, precision=lax.Precision.HIGHEST) + b_ref[...]


def _ada(c8, w_ada, b_ada):
    depth, d, n = w_ada.shape
    tn = 1536
    return pl.pallas_call(
        _ada_kernel,
        grid=(depth, n // tn),
        in_specs=[pl.BlockSpec((8, d), lambda l, j: (0, 0)),
                  pl.BlockSpec((None, d, tn), lambda l, j: (l, 0, j)),
                  pl.BlockSpec((None, 1, tn), lambda l, j: (l, 0, j))],
        out_specs=pl.BlockSpec((None, 8, tn), lambda l, j: (l, 0, j)),
        out_shape=jax.ShapeDtypeStruct((depth, 8, n), F32),
        compiler_params=_cparams(("arbitrary", "arbitrary")),
        name="ada_mod",
    )(c8, w_ada, b_ada.reshape(depth, 1, n))


def _mod_row(mod_ref, b, t, k):
    row = jnp.where(t == 0, 2, b)
    return mod_ref[pl.ds(row, 1), k * D_MODEL:(k + 1) * D_MODEL]


def _inproj_kernel(xc_ref, xl_ref, mod_ref, g_ref, w_ref, cos_ref, sin_ref,
                   qa_ref, ka_ref, va_ref, gb_ref, xb_ref, qs_ref, ks_ref, vs_ref, sg_ref):
    b, t = pl.program_id(0), pl.program_id(1)
    x = jnp.where(t == 0, xc_ref[...], xl_ref[...])
    h = _rms(x, g_ref[...]) * (1.0 + _mod_row(mod_ref, b, t, 1)) + _mod_row(mod_ref, b, t, 0)
    h = h.astype(BF16)

    def seg(lo, hi):
        return _dot(h, w_ref[:, lo:hi])

    qa_ref[...] = seg(_C_QA, _C_KA).astype(BF16)
    ka_ref[...] = seg(_C_KA, _C_VA).astype(BF16)
    va_ref[...] = seg(_C_VA, _C_GB).astype(BF16)
    gb_ref[...] = seg(_C_GB, _C_XB)
    xb_ref[...] = seg(_C_XB, _C_QS)
    cos = cos_ref[...]
    sin = sin_ref[...]
    half = ROPE_AXIS_DIM // 2
    lane = lax.broadcasted_iota(I32, (1, LANES), 1)
    first_half = (lane % ROPE_AXIS_DIM) < half

    def rope(v):
        partner = jnp.where(first_half, pltpu.roll(v, LANES - half, 1), pltpu.roll(v, half, 1))
        return (v * cos + partner * sin).astype(BF16)

    for j in range(4):
        qs_ref[:, j * LANES:(j + 1) * LANES] = rope(seg(_C_QS + j * LANES, _C_QS + (j + 1) * LANES))
    for j in range(2):
        ks_ref[:, j * LANES:(j + 1) * LANES] = rope(seg(_C_KS + j * LANES, _C_KS + (j + 1) * LANES))
    vs_ref[...] = seg(_C_VS, _C_GL).astype(BF16)
    for j in range(N_BRANCH):
        lo = _C_GL + j * D_MODEL
        sg_ref[:, j * D_MODEL:(j + 1) * D_MODEL] = jax.nn.sigmoid(seg(lo, lo + D_MODEL))


def _seq_specs(seq, t_off=0):
    xc, xl, lat_off = seq
    d = xc.shape[-1]
    return [pl.BlockSpec((None, TILE, d), lambda b, i: (b, 0, 0)),
            pl.BlockSpec((None, TILE, d), lambda b, i: (b, jnp.maximum(i + t_off - 1, 0) + lat_off, 0))]


def _seq_len(seq):
    return CTX_LEN + seq[1].shape[1] - seq[2] * TILE


def _inproj(seq, mod, g_pre, w_cat, cos_t, sin_t):
    bsz, d = seq[0].shape[0], seq[0].shape[-1]
    s = _seq_len(seq)
    nt = s // TILE
    tok = lambda b, t: (b, t, 0)
    const2 = lambda b, t: (0, 0)

    def out(width, dtype):
        return jax.ShapeDtypeStruct((bsz, s, width), dtype), pl.BlockSpec((None, TILE, width), tok)

    outs = [out(512, BF16), out(512, BF16), out(512, BF16), out(512, F32), out(512, F32),
            out(512, BF16), out(256, BF16), out(256, BF16), out(N_BRANCH * D_MODEL, F32)]
    return pl.pallas_call(
        _inproj_kernel,
        grid=(bsz, nt),
        in_specs=_seq_specs(seq) + [
                  pl.BlockSpec((8, 6 * d), const2),
                  pl.BlockSpec((1, d), const2),
                  pl.BlockSpec((d, _C_END), const2),
                  pl.BlockSpec((TILE, LANES), lambda b, t: (t, 0)),
                  pl.BlockSpec((TILE, LANES), lambda b, t: (t, 0))],
        out_specs=[o[1] for o in outs],
        out_shape=[o[0] for o in outs],
        compiler_params=_cparams(("arbitrary", "arbitrary")),
        name="inproj",
    )(seq[0], seq[1], mod, g_pre, w_cat, cos_t, sin_t)


def _two_head_attention(q, score_fn, value_fn):
    lane = lax.broadcasted_iota(I32, (1, LANES), 1)
    low = lane < HEAD_DIM
    outs = []
    for hh in range(2):
        qm = jnp.where(low if hh == 0 else jnp.logical_not(low), q, jnp.zeros_like(q))
        blocks = score_fn(qm, hh)
        m = blocks[0].max(axis=-1, keepdims=True)
        for s in blocks[1:]:
            m = jnp.maximum(m, s.max(axis=-1, keepdims=True))
        ps = [jnp.exp(s - m) for s in blocks]
        l = ps[0].sum(axis=-1, keepdims=True)
        for p in ps[1:]:
            l = l + p.sum(axis=-1, keepdims=True)
        o = value_fn([p.astype(BF16) for p in ps])
        outs.append(o / l)
    return jnp.where(low, outs[0], outs[1])


def _na_kernel(q_ref, k_ref, v_ref, tab_ref, o_ref, *, t_off, t_lat):
    t = pl.program_id(2) + t_off

    for pair in range(NA_PAIRS):
        lanes = slice(pair * LANES, (pair + 1) * LANES)
        q = q_ref[:, lanes]
        kc = k_ref[0:CTX_LEN, lanes]
        vc = v_ref[0:CTX_LEN, lanes]

        if t_off == 0:
            @pl.when(t == 0)
            def _ctx():
                o = _two_head_attention(q, lambda qm, hh: [_dot_t(qm, kc)], lambda ps: _dot(ps[0], vc))
                o_ref[:, lanes] = o.astype(BF16)

        @pl.when(t > 0)
        def _lat():
            nk = NA_KROWS * GRID_W
            start = jnp.clip((t - 2) * TILE, 0, t_lat - nk)
            start = pl.multiple_of(start + CTX_LEN, TILE)
            k = k_ref[pl.ds(start, nk), lanes]
            v = v_ref[pl.ds(start, nk), lanes]
            o = _two_head_attention(
                q,
                lambda qm, hh: [_dot_t(qm, k) + tab_ref[2 * pair + hh], _dot_t(qm, kc)],
                lambda ps: _dot(ps[0], v) + _dot(ps[1], vc))
            o_ref[:, lanes] = o.astype(BF16)


def _na_attention(qa, ka, va, table, with_ctx):
    bsz, s, _ = qa.shape
    t_lat = s - CTX_LEN
    nrb = t_lat // TILE
    t_off = 0 if with_ctx else 1
    steps = nrb + 1 - t_off
    out_rows = steps * TILE

    def var(i):
        rb = jnp.maximum(i + t_off - 1, 0)
        return jnp.where(rb == 0, 0, jnp.where(rb == nrb - 1, 2, 1))

    w = NA_PAIRS * LANES
    return pl.pallas_call(
        functools.partial(_na_kernel, t_off=t_off, t_lat=t_lat),
        grid=(bsz, NA_HEADS // (2 * NA_PAIRS), steps),
        in_specs=[pl.BlockSpec((None, TILE, w), lambda b, p, i: (b, i + t_off, p)),
                  pl.BlockSpec((None, s, w), lambda b, p, i: (b, 0, p)),
                  pl.BlockSpec((None, s, w), lambda b, p, i: (b, 0, p)),
                  pl.BlockSpec((None, 2 * NA_PAIRS, TILE, NA_KROWS * GRID_W), lambda b, p, i: (var(i), p, 0, 0))],
        out_specs=pl.BlockSpec((None, TILE, w), lambda b, p, i: (b, i, p)),
        out_shape=jax.ShapeDtypeStruct((bsz, out_rows, BRANCH_W), BF16),
        compiler_params=_cparams(("arbitrary", "arbitrary", "arbitrary")),
        name="na_attention",
    )(qa, ka, va, table)


def _na_row_geometry(rows):
    qr = np.arange(NA_QROWS)
    kr = np.arange(NA_KROWS)
    r0 = np.array([0, NA_WIN_R // 2, rows - NA_QROWS])
    kr0 = np.array([0, 0, rows - NA_KROWS])
    q_abs = r0[:, None] + qr[None, :]
    k_abs = kr0[:, None] + kr[None, :]
    rs = np.clip(q_abs - NA_WIN_R // 2, 0, rows - NA_WIN_R)
    rvalid = (k_abs[:, None, :] >= rs[:, :, None]) & (k_abs[:, None, :] < rs[:, :, None] + NA_WIN_R)
    ridx = np.clip(k_abs[:, None, :] - q_abs[:, :, None] + NA_WIN_R - 1, 0, 2 * NA_WIN_R - 2)
    return ridx, rvalid


def _na_table_kernel(col_ref, o_ref, *, ridx, rvalid):
    lane = lax.broadcasted_iota(I32, (GRID_W, LANES), 1)
    low = lane < GRID_W
    neg = jnp.full((GRID_W, LANES), NEG_INF, F32)
    for v in range(3):
        @pl.when(pl.program_id(0) == v)
        def _():
            for q in range(NA_QROWS):
                for m in range(NA_KROWS // 2):
                    halves = []
                    for k in (2 * m, 2 * m + 1):
                        halves.append(col_ref[int(ridx[v, q, k])] if rvalid[v, q, k] else neg)
                    o_ref[q * GRID_W:(q + 1) * GRID_W, m * LANES:(m + 1) * LANES] = jnp.where(low, halves[0], halves[1])


def _na_bias_table(rpb, rows):
    ridx, rvalid = _na_row_geometry(rows)
    qc = np.arange(GRID_W)
    kc = np.arange(GRID_W)
    q_start = np.clip(qc - NA_WIN_C // 2, 0, GRID_W - NA_WIN_C)
    cvalid = (kc[None, :] >= q_start[:, None]) & (kc[None, :] < q_start[:, None] + NA_WIN_C)
    cidx = np.clip(kc[None, :] - qc[:, None] + NA_WIN_C - 1, 0, 2 * NA_WIN_C - 2)
    nr, ncol = 2 * NA_WIN_R - 1, 2 * NA_WIN_C - 1
    csel = (np.where(cvalid, cidx, ncol)[..., None] == np.arange(ncol + 1)).astype(np.float32)
    rpb_ext = jnp.concatenate([rpb.astype(F32), jnp.full(rpb.shape[:2] + (1,), NEG_INF, F32)], axis=-1)
    col = jnp.einsum("hij,cdj->hicd", rpb_ext, jnp.asarray(csel), precision=lax.Precision.HIGHEST)
    col = jnp.concatenate([col, col], axis=-1)
    nk = NA_KROWS * GRID_W
    return pl.pallas_call(
        functools.partial(_na_table_kernel, ridx=ridx, rvalid=rvalid),
        grid=(3, NA_HEADS),
        in_specs=[pl.BlockSpec((None, nr, GRID_W, LANES), lambda v, h: (h, 0, 0, 0))],
        out_specs=pl.BlockSpec((None, None, TILE, nk), lambda v, h: (v, h, 0, 0)),
        out_shape=jax.ShapeDtypeStruct((3, NA_HEADS, TILE, nk), F32),
        compiler_params=_cparams(("arbitrary", "arbitrary")),
        name="na_bias_table",
    )(col)


def _sw_kernel(q_ref, k_ref, v_ref, mask_ref, sink_ref, o_ref, *, t_off, t_lat):
    t = pl.program_id(1) + t_off
    lane = lax.broadcasted_iota(I32, (1, LANES), 1)
    low = lane < HEAD_DIM
    zero = jnp.zeros((SW_BLOCK, LANES), BF16)
    n_ctx_tiles = CTX_LEN // SW_BLOCK

    for hk in range(SW_KV_HEADS):
        kv_lanes = slice(hk * LANES, (hk + 1) * LANES)
        q_off = hk * 2 * LANES
        q01 = q_ref[:, q_off:q_off + LANES]
        q23 = q_ref[:, q_off + LANES:q_off + 2 * LANES]
        qs = jnp.concatenate([jnp.where(low, q01, zero), jnp.where(low, zero, q01),
                              jnp.where(low, q23, zero), jnp.where(low, zero, q23)], axis=0)
        kc = k_ref[0:CTX_LEN, kv_lanes]
        vc = v_ref[0:CTX_LEN, kv_lanes]
        sink = sink_ref[hk]
        s_cx = _dot_t(qs, kc)

        def finish(blocks, vals, sink=sink, q_off=q_off):
            m = sink
            for s in blocks:
                m = jnp.maximum(m, s.max(axis=-1, keepdims=True))
            ps = [jnp.exp(s - m) for s in blocks]
            l = jnp.exp(sink - m)
            for p in ps:
                l = l + p.sum(axis=-1, keepdims=True)
            o = _dot(ps[0].astype(BF16), vals[0])
            for p, v in zip(ps[1:], vals[1:]):
                o = o + _dot(p.astype(BF16), v)
            o = o / l
            b = SW_BLOCK
            o_ref[:, q_off:q_off + LANES] = jnp.where(low, o[0:b], o[b:2 * b]).astype(BF16)
            o_ref[:, q_off + LANES:q_off + 2 * LANES] = jnp.where(low, o[2 * b:3 * b], o[3 * b:4 * b]).astype(BF16)

        if t_off == 0:
            @pl.when(t < n_ctx_tiles)
            def _ctx():
                finish([s_cx], [vc])

        @pl.when(t >= n_ctx_tiles)
        def _lat():
            nk = 3 * SW_BLOCK
            start = jnp.clip((t - n_ctx_tiles - 1) * SW_BLOCK, 0, t_lat - nk)
            start = pl.multiple_of(start + CTX_LEN, SW_BLOCK)
            k = k_ref[pl.ds(start, nk), kv_lanes]
            v = v_ref[pl.ds(start, nk), kv_lanes]
            finish([_dot_t(qs, k) + mask_ref[...], s_cx], [v, vc])


def _sw_attention(qs, ks, vs, mask, sink_col, with_ctx):
    bsz, s, _ = qs.shape
    t_lat = s - CTX_LEN
    nb = t_lat // SW_BLOCK
    n_ctx_tiles = CTX_LEN // SW_BLOCK
    t_off = 0 if with_ctx else n_ctx_tiles
    steps = nb + n_ctx_tiles - t_off

    def var(i):
        n = jnp.maximum(i + t_off - n_ctx_tiles, 0)
        return jnp.where(n == 0, 0, jnp.where(n == nb - 1, 2, 1))

    g = SW_HEADS // SW_KV_HEADS
    return pl.pallas_call(
        functools.partial(_sw_kernel, t_off=t_off, t_lat=t_lat),
        grid=(bsz, steps),
        in_specs=[pl.BlockSpec((None, SW_BLOCK, BRANCH_W), lambda b, i: (b, i + t_off, 0)),
                  pl.BlockSpec((None, s, SW_KV_HEADS * LANES), lambda b, i: (b, 0, 0)),
                  pl.BlockSpec((None, s, SW_KV_HEADS * LANES), lambda b, i: (b, 0, 0)),
                  pl.BlockSpec((None, g * SW_BLOCK, 3 * SW_BLOCK), lambda b, i: (var(i), 0, 0)),
                  pl.BlockSpec((SW_KV_HEADS, g * SW_BLOCK, 1), lambda b, i: (0, 0, 0))],
        out_specs=pl.BlockSpec((None, SW_BLOCK, BRANCH_W), lambda b, i: (b, i, 0)),
        out_shape=jax.ShapeDtypeStruct((bsz, steps * SW_BLOCK, BRANCH_W), BF16),
        compiler_params=_cparams(("arbitrary", "arbitrary")),
        name="sw_attention",
    )(qs, ks, vs, mask, sink_col)


def _sw_mask_table():
    i = np.arange(SW_BLOCK)[:, None]
    j = np.arange(3 * SW_BLOCK)[None, :]
    tabs = []
    for shift in (0, SW_BLOCK, 2 * SW_BLOCK):
        rel = j - i - shift
        tabs.append(np.where(np.abs(rel) <= SW_WINDOW, 0.0, NEG_INF).astype(np.float32))
    tab = np.stack(tabs)
    return jnp.asarray(np.tile(tab, (1, SW_HEADS // SW_KV_HEADS, 1)))


def _lru_kernel(x_ref, xp_ref, xn_ref, cw_ref, cb_ref, wg_ref, bg_ref, c8_ref, o_ref,
                xs, carry, *, nc):
    d, i = pl.program_id(1), pl.program_id(2)
    ci = jnp.where(d == 0, i, jnp.where(i == 0, 0, nc - i))
    tc = TILE
    w = LRU_WIDTH

    no_prev = (ci == 0) | (ci == 1)
    no_next = (ci == 0) | (ci == nc - 1)
    xs[0:8, :] = jnp.where(no_prev, 0.0, xp_ref[...])
    xs[8:8 + tc, :] = x_ref[...]
    xs[8 + tc:16 + tc, :] = jnp.where(no_next, 0.0, xn_ref[...])
    u = cb_ref[...] + cw_ref[0:1, :] * xs[7:7 + tc, :]
    for j in range(1, LRU_CONV):
        u = u + cw_ref[j:j + 1, :] * xs[7 + j:7 + j + tc, :]

    g = _dot(u.astype(BF16), wg_ref[...]) + bg_ref[...]
    r = jax.nn.sigmoid(g[:, :w])
    ig = jax.nn.sigmoid(g[:, w:])
    log_a = c8_ref[...] * r
    a = jnp.exp(log_a)
    th = jnp.tanh(log_a)
    bb = jnp.sqrt(-2.0 * th / (1.0 - th)) * (ig * u)
    @pl.when(i == 0)
    def _init():
        carry[...] = jnp.zeros_like(carry)

    ng = tc // 8
    a3 = a.reshape(ng, 8, w)
    b3 = bb.reshape(ng, 8, w)
    sub = lax.broadcasted_iota(I32, (ng, 8, w), 1)

    def scan(reverse):
        ag, bg_ = a3, b3
        for s in (1, 2, 4):
            shift = 8 - s if reverse else s
            keep = (sub < 8 - s) if reverse else (sub >= s)
            a_sh = pltpu.roll(ag, shift, 1)
            b_sh = pltpu.roll(bg_, shift, 1)
            bg_ = jnp.where(keep, ag * b_sh + bg_, bg_)
            ag = jnp.where(keep, ag * a_sh, ag)
        h = carry[0:1, :]
        for gi in (range(ng - 1, -1, -1) if reverse else range(ng)):
            hg = bg_[gi] + ag[gi] * h
            o_ref[gi * 8:(gi + 1) * 8, :] = hg
            h = hg[0:1, :] if reverse else hg[7:8, :]
        carry[0:1, :] = h

    @pl.when(d == 0)
    def _fwd():
        scan(False)

    @pl.when(d == 1)
    def _bwd():
        scan(True)


def _lru(xb, conv_w, conv_b, wg, bg, c8):
    bsz, s, w = xb.shape
    nc = s // TILE
    r8 = TILE // 8

    def chunk(d, i):
        return jnp.where(d == 0, i, jnp.where(i == 0, 0, nc - i))

    return pl.pallas_call(
        functools.partial(_lru_kernel, nc=nc),
        grid=(bsz, 2, nc),
        in_specs=[pl.BlockSpec((None, TILE, w), lambda b, d, i: (b, chunk(d, i), 0)),
                  pl.BlockSpec((None, 8, w), lambda b, d, i: (b, jnp.maximum(chunk(d, i) * r8 - 1, 0), 0)),
                  pl.BlockSpec((None, 8, w), lambda b, d, i: (b, jnp.minimum((chunk(d, i) + 1) * r8, s // 8 - 1), 0)),
                  pl.BlockSpec((LRU_CONV, w), lambda b, d, i: (0, 0)),
                  pl.BlockSpec((1, w), lambda b, d, i: (0, 0)),
                  pl.BlockSpec((None, w, 2 * w), lambda b, d, i: (d, 0, 0)),
                  pl.BlockSpec((None, 1, 2 * w), lambda b, d, i: (d, 0, 0)),
                  pl.BlockSpec((None, 1, w), lambda b, d, i: (d, 0, 0))],
        out_specs=pl.BlockSpec((None, None, TILE, w), lambda b, d, i: (d, b, chunk(d, i), 0)),
        out_shape=jax.ShapeDtypeStruct((2, bsz, s, w), F32),
        scratch_shapes=[pltpu.VMEM((TILE + 16, w), F32), pltpu.VMEM((8, w), F32)],
        compiler_params=_cparams(("arbitrary", "arbitrary", "arbitrary")),
        name="rglru",
    )(xb, xb, xb, conv_w, conv_b, wg, bg, c8)


def _gelu_tanh(x):
    return 0.5 * x * (1.0 + jnp.tanh(np.sqrt(2.0 / np.pi).astype(np.float32) * (x + 0.044715 * (x * x * x))))


def _merge_kernel(xc_ref, xl_ref, ya_ref, hf_ref, hb_ref, gb_ref, yc_ref, sg_ref, mod_ref, gpost_ref, gpre_ref,
                  wbr_ref, wout_ref, wr_ref, br_ref, cnt0_ref,
                  x1_ref, h2_ref, idx_ref, wgt_ref, rank_ref, cnt_ref, cnt_s, *, t_off):
    b, ti = pl.program_id(0), pl.program_id(1)
    t = ti + t_off
    d = D_MODEL

    @pl.when((b == 0) & (ti == 0))
    def _init():
        cnt_s[...] = cnt0_ref[...]

    yb = (_gelu_tanh(gb_ref[...]) * (hf_ref[...] + hb_ref[...])).astype(BF16)
    ys = (ya_ref[...], yb, yc_ref[...])
    mix = sg_ref[:, 0:d] * _dot(ys[0], wbr_ref[0])
    for k in range(1, N_BRANCH):
        mix = mix + sg_ref[:, k * d:(k + 1) * d] * _dot(ys[k], wbr_ref[k])
    y = _dot(mix.astype(BF16), wout_ref[...])
    x = jnp.where(t == 0, xc_ref[...], xl_ref[...])
    x1 = x + _mod_row(mod_ref, b, t, 2) * _rms(y, gpost_ref[...])
    x1_ref[...] = x1
    h2 = _rms(x1, gpre_ref[...]) * (1.0 + _mod_row(mod_ref, b, t, 4)) + _mod_row(mod_ref, b, t, 3)
    _store_token_rows(h2_ref, h2, TILE)

    logits = _dot(h2.astype(BF16), wr_ref[...]) + br_ref[...]
    lane = lax.broadcasted_iota(I32, (TILE, LANES), 1)
    work = logits
    tops, sels = [], []
    for k in range(TOP_K):
        m = work.max(axis=-1, keepdims=True)
        first = jnp.where(work == m, lane, LANES).min(axis=-1, keepdims=True)
        sel = lane == first
        work = jnp.where(sel, -3.0e38, work)
        tops.append(m)
        sels.append(sel)
    es = [jnp.exp(m - tops[0]) for m in tops]
    den = es[0] + es[1] + es[2] + es[3]

    chosen = jnp.zeros((TILE, LANES), F32)
    for sel in sels:
        chosen = jnp.where(sel, 1.0, chosen)
    rr = lax.broadcasted_iota(I32, (TILE, TILE), 0)
    cc = lax.broadcasted_iota(I32, (TILE, TILE), 1)
    tri = jnp.where(cc < rr, 1.0, 0.0).astype(BF16)
    pref = _dot(tri, chosen.astype(BF16)) + cnt_s[0:1, :]
    cnt_new = cnt_s[0:1, :] + chosen.sum(axis=0, keepdims=True)
    cnt_s[0:1, :] = cnt_new
    cnt_ref[...] = jnp.broadcast_to(cnt_new, cnt_ref.shape)

    idx_o = jnp.zeros((TILE, LANES), I32)
    wgt_o = jnp.zeros((TILE, LANES), F32)
    rank_o = jnp.zeros((TILE, LANES), I32)
    for k in range(TOP_K):
        first = jnp.where(sels[k], lane, 0).max(axis=-1, keepdims=True)
        rk = jnp.where(sels[k], pref, 0.0).sum(axis=-1, keepdims=True)
        idx_o = jnp.where(lane == k, first, idx_o)
        wgt_o = jnp.where(lane == k, es[k] / den, wgt_o)
        rank_o = jnp.where(lane == k, rk.astype(I32), rank_o)
    idx_ref[...] = idx_o
    wgt_ref[...] = wgt_o
    rank_ref[...] = rank_o


def _merge(seq, ya, hfb, gb, yc, sg, mod, g_post, g_pre, w_br, w_out, w_r, b_r, cnt0, with_ctx):
    bsz, d = seq[0].shape[0], seq[0].shape[-1]
    s = _seq_len(seq)
    t_off = 0 if with_ctx else 1
    nt = s // TILE - t_off
    rows = nt * TILE
    inp = lambda b, t: (b, t + t_off, 0)
    loc = lambda b, t: (b, t, 0)
    c2 = lambda b, t: (0, 0)
    c3 = lambda b, t: (0, 0, 0)

    def out(width, dtype):
        return jax.ShapeDtypeStruct((bsz, rows, width), dtype), pl.BlockSpec((None, TILE, width), loc)

    h2_out = (jax.ShapeDtypeStruct((bsz, rows * TOK_ROWS, LANES), F32),
              pl.BlockSpec((None, TILE * TOK_ROWS, LANES), loc))
    outs = [out(d, F32), h2_out, out(LANES, I32), out(LANES, F32), out(LANES, I32)]
    shapes = [o[0] for o in outs] + [jax.ShapeDtypeStruct((8, LANES), F32)]
    specs = [o[1] for o in outs] + [pl.BlockSpec((8, LANES), c2)]
    return pl.pallas_call(
        functools.partial(_merge_kernel, t_off=t_off),
        grid=(bsz, nt),
        in_specs=_seq_specs(seq, t_off) + [
                  pl.BlockSpec((None, TILE, BRANCH_W), loc),
                  pl.BlockSpec((None, None, TILE, BRANCH_W), lambda b, t: (0, b, t + t_off, 0)),
                  pl.BlockSpec((None, None, TILE, BRANCH_W), lambda b, t: (1, b, t + t_off, 0)),
                  pl.BlockSpec((None, TILE, BRANCH_W), inp),
                  pl.BlockSpec((None, TILE, BRANCH_W), loc),
                  pl.BlockSpec((None, TILE, N_BRANCH * d), inp),
                  pl.BlockSpec((8, 6 * d), c2),
                  pl.BlockSpec((1, d), c2),
                  pl.BlockSpec((1, d), c2),
                  pl.BlockSpec((N_BRANCH, BRANCH_W, d), c3),
                  pl.BlockSpec((d, d), c2),
                  pl.BlockSpec((d, LANES), c2),
                  pl.BlockSpec((1, LANES), c2),
                  pl.BlockSpec((8, LANES), c2)],
        out_specs=specs,
        out_shape=shapes,
        scratch_shapes=[pltpu.VMEM((8, LANES), F32)],
        compiler_params=_cparams(("arbitrary", "arbitrary")),
        name="merge_router",
    )(seq[0], seq[1], ya, hfb, hfb, gb, yc, sg, mod, g_post, g_pre, w_br, w_out, w_r, b_r, cnt0)


def _store_token_rows(ref, val, n, lead=()):
    for j in range(TOK_ROWS):
        ref[lead + (pl.ds(j, n, stride=TOK_ROWS), slice(None))] = val[:, j * LANES:(j + 1) * LANES]


def _load_token_rows(ref, n, lead=()):
    return jnp.concatenate([ref[lead + (pl.ds(j, n, stride=TOK_ROWS), slice(None))] for j in range(TOK_ROWS)], axis=1)


def _token_slice(i):
    return pl.ds(pl.multiple_of(i * TOK_ROWS, TOK_ROWS), TOK_ROWS)


def _dispatch_kernel(pos_ref, last_ref, h_ref, xs_ref, zbuf, sem, zsem, *, n_steps):
    step = pl.program_id(0)

    @pl.when(step == 0)
    def _zero_fill():
        zbuf[...] = jnp.zeros_like(zbuf)

        def fill(e):
            dst = xs_ref.at[pl.ds(pl.multiple_of(last_ref[0, e] * TOK_ROWS, TOK_ROWS), MOE_ROWS * TOK_ROWS)]
            return pltpu.make_async_copy(zbuf, dst, zsem)

        for e in range(N_EXPERTS):
            @pl.when(last_ref[0, e] >= 0)
            def _():
                fill(e).start()
        for e in range(N_EXPERTS):
            @pl.when(last_ref[0, e] >= 0)
            def _():
                fill(e).wait()

    base = step * TILE

    def row_copy(i, k):
        p = pos_ref[0, 0, i * TOP_K + k]
        return pltpu.make_async_copy(h_ref.at[_token_slice(base + i)], xs_ref.at[_token_slice(p)], sem)

    def start(i, c):
        for k in range(TOP_K):
            row_copy(i, k).start(priority=k % 2)
        return c

    lax.fori_loop(0, TILE, start, 0, unroll=DMA_UNROLL)

    def drain():
        whole_tile = pl.ds(0, TILE * TOK_ROWS)
        for _ in range(TOP_K):
            pltpu.make_async_copy(h_ref.at[whole_tile], xs_ref.at[whole_tile], sem).wait()

    @pl.when(step > 0)
    def _():
        drain()

    @pl.when(step == n_steps - 1)
    def _():
        drain()


def _dispatch(h2, pos, last_tile_row, n_rows):
    bsz, rows = pos.shape[:2]
    n_steps = bsz * rows // TILE
    return pl.pallas_call(
        functools.partial(_dispatch_kernel, n_steps=n_steps),
        grid=(n_steps,),
        in_specs=[pl.BlockSpec((1, 1, TILE * TOP_K), lambda i: (i, 0, 0), memory_space=pltpu.SMEM),
                  pl.BlockSpec((1, N_EXPERTS), lambda i: (0, 0), memory_space=pltpu.SMEM),
                  pl.BlockSpec(memory_space=pl.ANY)],
        out_specs=pl.BlockSpec(memory_space=pl.ANY),
        out_shape=jax.ShapeDtypeStruct((n_rows * TOK_ROWS, LANES), F32),
        scratch_shapes=[pltpu.VMEM((MOE_ROWS * TOK_ROWS, LANES), F32), pltpu.SemaphoreType.DMA(()),
                        pltpu.SemaphoreType.DMA(())],
        compiler_params=_cparams(("arbitrary",)),
        name="moe_dispatch",
    )(pos.reshape(n_steps, 1, TILE * TOP_K), last_tile_row, h2.reshape(bsz * rows * TOK_ROWS, LANES))


def _expert_kernel(te_ref, nu_ref, x_ref, w1_ref, b1_ref, w2_ref, b2_ref, y_ref, w1_s, w2_s):
    t = pl.program_id(0)
    prev = te_ref[jnp.maximum(t - 1, 0)]
    active = t < nu_ref[0]

    @pl.when(active & ((t == 0) | (te_ref[t] != prev)))
    def _load():
        w1_s[...] = w1_ref[...].astype(BF16)
        w2_s[...] = w2_ref[...].astype(BF16)

    @pl.when(active)
    def _run():
        f = D_EXPERT
        x = _load_token_rows(x_ref, MOE_ROWS).astype(BF16)
        hid = _dot(x, w1_s[...]) + b1_ref[...]
        glu = jnp.minimum(hid[:, :f], SWIGLU_LIMIT)
        lin = jnp.clip(hid[:, f:], -SWIGLU_LIMIT, SWIGLU_LIMIT)
        act = glu * jax.nn.sigmoid(SWIGLU_ALPHA * glu) * (lin + 1.0)
        _store_token_rows(y_ref, _dot(act.astype(BF16), w2_s[...]) + b2_ref[...], MOE_ROWS)

    @pl.when(jnp.logical_not(active))
    def _idle():
        y_ref[...] = jnp.zeros_like(y_ref)


def _experts(xs, tile_expert, n_used, w1, b1, w2, b2, layer):
    depth, e, d, f2 = w1.shape
    n_tiles = xs.shape[0] // (MOE_ROWS * TOK_ROWS)
    blk = (MOE_ROWS * TOK_ROWS, LANES)
    xmap = lambda t, te, nu: (jnp.minimum(t, nu[0] - 1), 0)
    wmap = lambda t, te, nu: (layer, te[t], 0, 0)
    grid_spec = pltpu.PrefetchScalarGridSpec(
        num_scalar_prefetch=2,
        grid=(n_tiles,),
        in_specs=[pl.BlockSpec(blk, xmap),
                  pl.BlockSpec((None, None, d, f2), wmap),
                  pl.BlockSpec((None, None, 1, f2), wmap),
                  pl.BlockSpec((None, None, f2 // 2, d), wmap),
                  pl.BlockSpec((None, None, 1, d), wmap)],
        out_specs=pl.BlockSpec(blk, lambda t, te, nu: (t, 0)),
        scratch_shapes=[pltpu.VMEM((d, f2), BF16), pltpu.VMEM((f2 // 2, d), BF16)])
    return pl.pallas_call(
        _expert_kernel,
        grid_spec=grid_spec,
        out_shape=jax.ShapeDtypeStruct(xs.shape, F32),
        compiler_params=_cparams(("arbitrary",)),
        name="moe_experts",
    )(tile_expert, n_used, xs, w1, b1.reshape(depth, e, 1, f2), w2, b2.reshape(depth, e, 1, d))


def _combine_kernel(pos_ref, posn_ref, ys_ref, wgt_ref, x1_ref, mod_ref, gpost_ref, o_ref, buf, sem, *, t_off, nt):
    b, ti = pl.program_id(0), pl.program_id(1)
    t = ti + t_off
    step = b * nt + ti
    n_steps = pl.num_programs(0) * nt
    slot = step % 2

    def issue(p_ref, sl):
        def start(i, c):
            for k in range(TOP_K):
                p = p_ref[0, 0, i * TOP_K + k]
                pltpu.make_async_copy(ys_ref.at[_token_slice(p)], buf.at[sl, k, _token_slice(i)],
                                      sem.at[sl]).start(priority=k % 2)
            return c
        lax.fori_loop(0, TILE, start, 0, unroll=DMA_UNROLL)

    @pl.when(step == 0)
    def _():
        issue(pos_ref, 0)

    @pl.when(step + 1 < n_steps)
    def _():
        issue(posn_ref, 1 - slot)

    for k in range(TOP_K):
        pltpu.make_async_copy(ys_ref.at[pl.ds(0, TILE * TOK_ROWS)], buf.at[slot, k], sem.at[slot]).wait()

    wgt = wgt_ref[...]
    moe = wgt[:, 0:1] * _load_token_rows(buf, TILE, (slot, 0))
    for k in range(1, TOP_K):
        moe = moe + wgt[:, k:k + 1] * _load_token_rows(buf, TILE, (slot, k))
    o_ref[...] = x1_ref[...] + _mod_row(mod_ref, b, t, 5) * _rms(moe, gpost_ref[...])


def _combine(ys, pos, wgt, x1, mod, g_post, with_ctx):
    bsz, rows, d = x1.shape
    nt = rows // TILE
    tok = lambda b, t: (b, t, 0)
    last = bsz * nt - 1
    return pl.pallas_call(
        functools.partial(_combine_kernel, t_off=0 if with_ctx else 1, nt=nt),
        grid=(bsz, nt),
        in_specs=[pl.BlockSpec((1, 1, TILE * TOP_K), lambda b, t: (b * nt + t, 0, 0), memory_space=pltpu.SMEM),
                  pl.BlockSpec((1, 1, TILE * TOP_K), lambda b, t: (jnp.minimum(b * nt + t + 1, last), 0, 0),
                               memory_space=pltpu.SMEM),
                  pl.BlockSpec(memory_space=pl.ANY),
                  pl.BlockSpec((None, TILE, LANES), tok),
                  pl.BlockSpec((None, TILE, d), tok),
                  pl.BlockSpec((8, 6 * d), lambda b, t: (0, 0)),
                  pl.BlockSpec((1, d), lambda b, t: (0, 0))],
        out_specs=pl.BlockSpec((None, TILE, d), tok),
        out_shape=jax.ShapeDtypeStruct((bsz, rows, d), F32),
        scratch_shapes=[pltpu.VMEM((2, TOP_K, TILE * TOK_ROWS, LANES), F32), pltpu.SemaphoreType.DMA((2,))],
        compiler_params=_cparams(("arbitrary", "arbitrary")),
        name="moe_combine",
    )(pos.reshape(bsz * nt, 1, TILE * TOP_K), pos.reshape(bsz * nt, 1, TILE * TOP_K), ys, wgt, x1, mod, g_post)


def _moe(h2, idx, wgt, rank, counts, x1, mod, g_post, w1, b1, w2, b2, layer, with_ctx):
    bsz, rows, d = x1.shape
    n_tiles = (bsz * rows * TOP_K) // MOE_ROWS + N_EXPERTS
    n_rows = n_tiles * MOE_ROWS
    cnt = counts.astype(I32)
    tiles_e = (cnt + MOE_ROWS - 1) // MOE_ROWS
    tile_end = jnp.cumsum(tiles_e)
    row_start = (tile_end - tiles_e) * MOE_ROWS
    n_used = tile_end[-1:]
    tile_ids = jnp.minimum(jnp.arange(n_tiles, dtype=I32), n_used[0] - 1)
    tile_expert = jnp.sum((tile_end[None, :] <= tile_ids[:, None]).astype(I32), axis=1)
    tile_expert = jnp.minimum(tile_expert, N_EXPERTS - 1)
    experts = jnp.arange(N_EXPERTS, dtype=I32)
    start_of = jnp.sum(jnp.where(idx[..., :TOP_K, None] == experts, row_start, 0), axis=-1)
    pos = (start_of + rank[..., :TOP_K]).astype(I32)
    last_tile_row = jnp.where(tiles_e > 0, (tile_end - 1) * MOE_ROWS, -1).astype(I32)[None, :]
    xs = _dispatch(h2, pos, last_tile_row, n_rows)
    ys = _experts(xs, tile_expert, n_used.astype(I32), w1, b1, w2, b2, layer)
    return _combine(ys, pos, wgt, x1, mod, g_post, with_ctx)


def _dup_heads(w):
    d = w.shape[0]
    w = w.reshape(d, -1, 1, HEAD_DIM)
    return jnp.broadcast_to(w, (d, w.shape[1], 2, HEAD_DIM)).reshape(d, -1)


def _inproj_weight(w_in):
    sizes = (BRANCH_W, BRANCH_W, BRANCH_W, LRU_WIDTH, LRU_WIDTH, BRANCH_W,
             SW_KV_HEADS * HEAD_DIM, SW_KV_HEADS * HEAD_DIM, N_BRANCH * D_MODEL)
    offs = np.cumsum((0,) + sizes)
    qa, ka, va, gb, xb, qs, ks, vs, gl = [w_in[:, offs[i]:offs[i + 1]] for i in range(9)]
    scale = HEAD_DIM ** -0.5
    cols = [qa * scale, ka, va, gb, xb, qs * scale, _dup_heads(ks), _dup_heads(vs), gl]
    return jnp.concatenate(cols, axis=1).astype(BF16)


def _rope_tables(t_lat):
    pos = np.arange(t_lat)
    row = (pos // GRID_W).astype(np.float32)
    col = (pos % GRID_W).astype(np.float32)
    half = ROPE_AXIS_DIM // 2
    inv = (ROPE_BASE ** (-jnp.arange(0, ROPE_AXIS_DIM, 2, dtype=F32) / ROPE_AXIS_DIM))
    d = np.arange(HEAD_DIM)
    axis = d // ROPE_AXIS_DIM
    freq = d % half
    sign = np.where((d % ROPE_AXIS_DIM) < half, -1.0, 1.0).astype(np.float32)
    p = jnp.where(jnp.asarray(axis == 0)[None, :], jnp.asarray(row)[:, None], jnp.asarray(col)[:, None])
    ang = p * inv[freq][None, :]
    cos = jnp.concatenate([jnp.ones((CTX_LEN, HEAD_DIM), F32), jnp.cos(ang)], axis=0)
    sin = jnp.concatenate([jnp.zeros((CTX_LEN, HEAD_DIM), F32), jnp.sin(ang) * sign[None, :]], axis=0)
    return jnp.tile(cos, (1, 2)), jnp.tile(sin, (1, 2))


def _lru_gate_weights(wa, ba, wx, bx):
    def dense(wblk):
        k, bs = wblk.shape[1], wblk.shape[2]
        eye = jnp.eye(k, dtype=wblk.dtype)
        return jnp.einsum("dkij,kl->dkilj", wblk, eye).reshape(2, k * bs, k * bs)
    wg = jnp.concatenate([dense(wa), dense(wx)], axis=-1).astype(BF16)
    bg = jnp.concatenate([ba, bx], axis=-1)[:, None, :].astype(F32)
    return wg, bg


def _layer(seq, mod, g_mix_pre, g_mix_post, g_ffn_pre, g_ffn_post, w_in, na_rpb, conv_w, conv_b,
           lru_wa, lru_ba, lru_wx, lru_bx, lru_lam, sw_sinks, w_branch, w_out, w_router, b_router,
           w1, b1, w2, b2, layer, with_ctx):
    d = seq[0].shape[-1]
    t_lat = _seq_len(seq) - CTX_LEN
    row = lambda v: v.reshape(1, -1).astype(F32)

    cos_t, sin_t = _rope_tables(t_lat)
    qa, ka, va, gb, xb, qs, ks, vs, sg = _inproj(seq, mod, row(g_mix_pre), _inproj_weight(w_in), cos_t, sin_t)

    ya = _na_attention(qa, ka, va, _na_bias_table(na_rpb, t_lat // GRID_W), with_ctx)

    wg, bg = _lru_gate_weights(lru_wa, lru_ba, lru_wx, lru_bx)
    c8 = (-LRU_C * jax.nn.softplus(-lru_lam.astype(F32)))[:, None, :]
    hfb = _lru(xb, conv_w.astype(F32), row(conv_b), wg, bg, c8)

    g = SW_HEADS // SW_KV_HEADS
    sink_col = jnp.repeat(sw_sinks.astype(F32).reshape(SW_KV_HEADS, g), SW_BLOCK, axis=1)[:, :, None]
    yc = _sw_attention(qs, ks, vs, _sw_mask_table(), sink_col, with_ctx)

    w_r = jnp.zeros((d, LANES), F32).at[:, :N_EXPERTS].set(w_router).astype(BF16)
    b_r = jnp.full((1, LANES), NEG_INF, F32).at[0, :N_EXPERTS].set(b_router.astype(F32))
    x1, h2, idx, wgt, rank, cnt = _merge(
        seq, ya, hfb, gb, yc, sg, mod, row(g_mix_post), row(g_ffn_pre), w_branch.astype(BF16),
        w_out.astype(BF16), w_r, b_r, jnp.zeros((8, LANES), F32), with_ctx)

    return _moe(h2, idx, wgt, rank, cnt[0, :N_EXPERTS], x1, mod, row(g_ffn_post), w1, b1, w2, b2, layer, with_ctx)


def kernel(x, c, ctx, c_ctx, w_ada, b_ada, g_mix_pre, g_mix_post, g_ffn_pre, g_ffn_post, w_in, na_rpb, conv_w, conv_b, lru_wa, lru_ba, lru_wx, lru_bx, lru_lam, sw_sinks, w_branch, w_out, w_router, b_router, w1, b1, w2, b2):
    depth = w_ada.shape[0]
    bsz, t_lat, d = x.shape
    assert bsz == 2 and d == D_MODEL and ctx.shape[1] == CTX_LEN
    assert t_lat % TILE == 0 and t_lat // GRID_W >= NA_KROWS and t_lat // SW_BLOCK >= 3
    c8 = jnp.zeros((8, d), F32).at[:bsz].set(c).at[bsz].set(c_ctx)
    mods = _ada(c8, w_ada, b_ada)
    seq = (ctx, x, 0)
    for l in range(depth):
        with_ctx = l < depth - 1
        out = _layer(seq, mods[l], g_mix_pre[l], g_mix_post[l], g_ffn_pre[l], g_ffn_post[l], w_in[l], na_rpb[l],
                     conv_w[l], conv_b[l], lru_wa[l], lru_ba[l], lru_wx[l], lru_bx[l], lru_lam[l], sw_sinks[l],
                     w_branch[l], w_out[l], w_router[l], b_router[l], w1, b1, w2, b2, l, with_ctx)
        seq = (out, out, 1)
    return out
```

```python
import functools

import numpy as np
import jax
import jax.numpy as jnp
from jax import lax
from jax.experimental import pallas as pl
from jax.experimental.pallas import tpu as pltpu

F32 = jnp.float32
BF16 = jnp.bfloat16
I32 = jnp.int32

D_MODEL = 1024
CTX_LEN = 256
GRID_W = 64
HEAD_DIM = 64
BRANCH_W = 512
N_BRANCH = 3
NA_HEADS = 8
NA_WIN_R = 8
NA_WIN_C = 16
LRU_WIDTH = 512
LRU_BLOCKS = 8
LRU_CONV = 4
LRU_C = 8.0
SW_HEADS = 8
SW_KV_HEADS = 2
SW_WINDOW = 128
SW_BLOCK = 128
ROPE_BASE = 10000.0
ROPE_AXIS_DIM = HEAD_DIM // 2
N_EXPERTS = 32
TOP_K = 4
D_EXPERT = 1024
SWIGLU_LIMIT = 7.0
SWIGLU_ALPHA = 1.702
RMS_EPS = 1e-6
NEG_INF = -1e30

LANES = 128
TILE = 256
NA_QROWS = 4
NA_KROWS = 12
NA_PAIRS = 2
MOE_ROWS = 512
TOK_ROWS = D_MODEL // LANES
DMA_UNROLL = 8
VMEM_LIMIT = 56 * 1024 * 1024

_C_QA, _C_KA, _C_VA, _C_GB, _C_XB, _C_QS, _C_KS, _C_VS, _C_GL, _C_END = (
    0, 512, 1024, 1536, 2048, 2560, 3072, 3328, 3584, 6656)


def _cparams(sem, vmem=VMEM_LIMIT):
    return pltpu.CompilerParams(dimension_semantics=sem, vmem_limit_bytes=vmem)


def _dot(a, b):
    return jnp.dot(a, b, preferred_element_type=F32)


def _dot_t(a, b):
    return lax.dot_general(a, b, (((1,), (1,)), ((), ())), preferred_element_type=F32)


def _rms(x, g):
    return x * lax.rsqrt(jnp.mean(x * x, axis=-1, keepdims=True) + RMS_EPS) * g


def _ada_kernel(c_ref, w_ref, b_ref, o_ref):
    c = c_ref[...]
    s = c * jax.nn.sigmoid(c)
    o_ref[...] = jnp.dot(s, w_ref[...], preferred_element_type=F32, precision=lax.Precision.HIGHEST) + b_ref[...]


def _ada(c8, w_ada, b_ada):
    depth, d, n = w_ada.shape
    tn = 1536
    return pl.pallas_call(
        _ada_kernel,
        grid=(depth, n // tn),
        in_specs=[pl.BlockSpec((8, d), lambda l, j: (0, 0)),
                  pl.BlockSpec((None, d, tn), lambda l, j: (l, 0, j)),
                  pl.BlockSpec((None, 1, tn), lambda l, j: (l, 0, j))],
        out_specs=pl.BlockSpec((None, 8, tn), lambda l, j: (l, 0, j)),
        out_shape=jax.ShapeDtypeStruct((depth, 8, n), F32),
        compiler_params=_cparams(("arbitrary", "arbitrary")),
        name="ada_mod",
    )(c8, w_ada, b_ada.reshape(depth, 1, n))


def _mod_row(mod_ref, b, t, k):
    row = jnp.where(t == 0, 2, b)
    return mod_ref[pl.ds(row, 1), k * D_MODEL:(k + 1) * D_MODEL]


def _inproj_kernel(xc_ref, xl_ref, mod_ref, g_ref, w_ref, cos_ref, sin_ref,
                   qa_ref, ka_ref, va_ref, gb_ref, xb_ref, qs_ref, ks_ref, vs_ref, sg_ref):
    b, t = pl.program_id(0), pl.program_id(1)
    x = jnp.where(t == 0, xc_ref[...], xl_ref[...])
    h = _rms(x, g_ref[...]) * (1.0 + _mod_row(mod_ref, b, t, 1)) + _mod_row(mod_ref, b, t, 0)
    h = h.astype(BF16)

    def seg(lo, hi):
        return _dot(h, w_ref[:, lo:hi])

    qa_ref[...] = seg(_C_QA, _C_KA).astype(BF16)
    ka_ref[...] = seg(_C_KA, _C_VA).astype(BF16)
    va_ref[...] = seg(_C_VA, _C_GB).astype(BF16)
    gb_ref[...] = seg(_C_GB, _C_XB)
    xb_ref[...] = seg(_C_XB, _C_QS)
    cos = cos_ref[...]
    sin = sin_ref[...]
    half = ROPE_AXIS_DIM // 2
    lane = lax.broadcasted_iota(I32, (1, LANES), 1)
    first_half = (lane % ROPE_AXIS_DIM) < half

    def rope(v):
        partner = jnp.where(first_half, pltpu.roll(v, LANES - half, 1), pltpu.roll(v, half, 1))
        return (v * cos + partner * sin).astype(BF16)

    for j in range(4):
        qs_ref[:, j * LANES:(j + 1) * LANES] = rope(seg(_C_QS + j * LANES, _C_QS + (j + 1) * LANES))
    for j in range(2):
        ks_ref[:, j * LANES:(j + 1) * LANES] = rope(seg(_C_KS + j * LANES, _C_KS + (j + 1) * LANES))
    vs_ref[...] = seg(_C_VS, _C_GL).astype(BF16)
    for j in range(N_BRANCH):
        lo = _C_GL + j * D_MODEL
        sg_ref[:, j * D_MODEL:(j + 1) * D_MODEL] = jax.nn.sigmoid(seg(lo, lo + D_MODEL))


def _seq_specs(seq, t_off=0):
    xc, xl, lat_off = seq
    d = xc.shape[-1]
    return [pl.BlockSpec((None, TILE, d), lambda b, i: (b, 0, 0)),
            pl.BlockSpec((None, TILE, d), lambda b, i: (b, jnp.maximum(i + t_off - 1, 0) + lat_off, 0))]


def _seq_len(seq):
    return CTX_LEN + seq[1].shape[1] - seq[2] * TILE


def _inproj(seq, mod, g_pre, w_cat, cos_t, sin_t):
    bsz, d = seq[0].shape[0], seq[0].shape[-1]
    s = _seq_len(seq)
    nt = s // TILE
    tok = lambda b, t: (b, t, 0)
    const2 = lambda b, t: (0, 0)

    def out(width, dtype):
        return jax.ShapeDtypeStruct((bsz, s, width), dtype), pl.BlockSpec((None, TILE, width), tok)

    outs = [out(512, BF16), out(512, BF16), out(512, BF16), out(512, F32), out(512, F32),
            out(512, BF16), out(256, BF16), out(256, BF16), out(N_BRANCH * D_MODEL, F32)]
    return pl.pallas_call(
        _inproj_kernel,
        grid=(bsz, nt),
        in_specs=_seq_specs(seq) + [
                  pl.BlockSpec((8, 6 * d), const2),
                  pl.BlockSpec((1, d), const2),
                  pl.BlockSpec((d, _C_END), const2),
                  pl.BlockSpec((TILE, LANES), lambda b, t: (t, 0)),
                  pl.BlockSpec((TILE, LANES), lambda b, t: (t, 0))],
        out_specs=[o[1] for o in outs],
        out_shape=[o[0] for o in outs],
        compiler_params=_cparams(("arbitrary", "arbitrary")),
        name="inproj",
    )(seq[0], seq[1], mod, g_pre, w_cat, cos_t, sin_t)


def _two_head_attention(q, score_fn, value_fn):
    lane = lax.broadcasted_iota(I32, (1, LANES), 1)
    low = lane < HEAD_DIM
    outs = []
    for hh in range(2):
        qm = jnp.where(low if hh == 0 else jnp.logical_not(low), q, jnp.zeros_like(q))
        blocks = score_fn(qm, hh)
        m = blocks[0].max(axis=-1, keepdims=True)
        for s in blocks[1:]:
            m = jnp.maximum(m, s.max(axis=-1, keepdims=True))
        ps = [jnp.exp(s - m) for s in blocks]
        l = ps[0].sum(axis=-1, keepdims=True)
        for p in ps[1:]:
            l = l + p.sum(axis=-1, keepdims=True)
        o = value_fn([p.astype(BF16) for p in ps])
        outs.append(o / l)
    return jnp.where(low, outs[0], outs[1])


def _na_kernel(q_ref, k_ref, v_ref, tab_ref, o_ref, *, t_off, t_lat):
    t = pl.program_id(2) + t_off

    for pair in range(NA_PAIRS):
        lanes = slice(pair * LANES, (pair + 1) * LANES)
        q = q_ref[:, lanes]
        kc = k_ref[0:CTX_LEN, lanes]
        vc = v_ref[0:CTX_LEN, lanes]

        if t_off == 0:
            @pl.when(t == 0)
            def _ctx():
                o = _two_head_attention(q, lambda qm, hh: [_dot_t(qm, kc)], lambda ps: _dot(ps[0], vc))
                o_ref[:, lanes] = o.astype(BF16)

        @pl.when(t > 0)
        def _lat():
            nk = NA_KROWS * GRID_W
            start = jnp.clip((t - 2) * TILE, 0, t_lat - nk)
            start = pl.multiple_of(start + CTX_LEN, TILE)
            k = k_ref[pl.ds(start, nk), lanes]
            v = v_ref[pl.ds(start, nk), lanes]
            o = _two_head_attention(
                q,
                lambda qm, hh: [_dot_t(qm, k) + tab_ref[2 * pair + hh], _dot_t(qm, kc)],
                lambda ps: _dot(ps[0], v) + _dot(ps[1], vc))
            o_ref[:, lanes] = o.astype(BF16)


def _na_attention(qa, ka, va, table, with_ctx):
    bsz, s, _ = qa.shape
    t_lat = s - CTX_LEN
    nrb = t_lat // TILE
    t_off = 0 if with_ctx else 1
    steps = nrb + 1 - t_off
    out_rows = steps * TILE

    def var(i):
        rb = jnp.maximum(i + t_off - 1, 0)
        return jnp.where(rb == 0, 0, jnp.where(rb == nrb - 1, 2, 1))

    w = NA_PAIRS * LANES
    return pl.pallas_call(
        functools.partial(_na_kernel, t_off=t_off, t_lat=t_lat),
        grid=(bsz, NA_HEADS // (2 * NA_PAIRS), steps),
        in_specs=[pl.BlockSpec((None, TILE, w), lambda b, p, i: (b, i + t_off, p)),
                  pl.BlockSpec((None, s, w), lambda b, p, i: (b, 0, p)),
                  pl.BlockSpec((None, s, w), lambda b, p, i: (b, 0, p)),
                  pl.BlockSpec((None, 2 * NA_PAIRS, TILE, NA_KROWS * GRID_W), lambda b, p, i: (var(i), p, 0, 0))],
        out_specs=pl.BlockSpec((None, TILE, w), lambda b, p, i: (b, i, p)),
        out_shape=jax.ShapeDtypeStruct((bsz, out_rows, BRANCH_W), BF16),
        compiler_params=_cparams(("arbitrary", "arbitrary", "arbitrary")),
        name="na_attention",
    )(qa, ka, va, table)


def _na_row_geometry(rows):
    qr = np.arange(NA_QROWS)
    kr = np.arange(NA_KROWS)
    r0 = np.array([0, NA_WIN_R // 2, rows - NA_QROWS])
    kr0 = np.array([0, 0, rows - NA_KROWS])
    q_abs = r0[:, None] + qr[None, :]
    k_abs = kr0[:, None] + kr[None, :]
    rs = np.clip(q_abs - NA_WIN_R // 2, 0, rows - NA_WIN_R)
    rvalid = (k_abs[:, None, :] >= rs[:, :, None]) & (k_abs[:, None, :] < rs[:, :, None] + NA_WIN_R)
    ridx = np.clip(k_abs[:, None, :] - q_abs[:, :, None] + NA_WIN_R - 1, 0, 2 * NA_WIN_R - 2)
    return ridx, rvalid


def _na_table_kernel(col_ref, o_ref, *, ridx, rvalid):
    lane = lax.broadcasted_iota(I32, (GRID_W, LANES), 1)
    low = lane < GRID_W
    neg = jnp.full((GRID_W, LANES), NEG_INF, F32)
    for v in range(3):
        @pl.when(pl.program_id(0) == v)
        def _():
            for q in range(NA_QROWS):
                for m in range(NA_KROWS // 2):
                    halves = []
                    for k in (2 * m, 2 * m + 1):
                        halves.append(col_ref[int(ridx[v, q, k])] if rvalid[v, q, k] else neg)
                    o_ref[q * GRID_W:(q + 1) * GRID_W, m * LANES:(m + 1) * LANES] = jnp.where(low, halves[0], halves[1])


def _na_bias_table(rpb, rows):
    ridx, rvalid = _na_row_geometry(rows)
    qc = np.arange(GRID_W)
    kc = np.arange(GRID_W)
    q_start = np.clip(qc - NA_WIN_C // 2, 0, GRID_W - NA_WIN_C)
    cvalid = (kc[None, :] >= q_start[:, None]) & (kc[None, :] < q_start[:, None] + NA_WIN_C)
    cidx = np.clip(kc[None, :] - qc[:, None] + NA_WIN_C - 1, 0, 2 * NA_WIN_C - 2)
    nr, ncol = 2 * NA_WIN_R - 1, 2 * NA_WIN_C - 1
    csel = (np.where(cvalid, cidx, ncol)[..., None] == np.arange(ncol + 1)).astype(np.float32)
    rpb_ext = jnp.concatenate([rpb.astype(F32), jnp.full(rpb.shape[:2] + (1,), NEG_INF, F32)], axis=-1)
    col = jnp.einsum("hij,cdj->hicd", rpb_ext, jnp.asarray(csel), precision=lax.Precision.HIGHEST)
    col = jnp.concatenate([col, col], axis=-1)
    nk = NA_KROWS * GRID_W
    return pl.pallas_call(
        functools.partial(_na_table_kernel, ridx=ridx, rvalid=rvalid),
        grid=(3, NA_HEADS),
        in_specs=[pl.BlockSpec((None, nr, GRID_W, LANES), lambda v, h: (h, 0, 0, 0))],
        out_specs=pl.BlockSpec((None, None, TILE, nk), lambda v, h: (v, h, 0, 0)),
        out_shape=jax.ShapeDtypeStruct((3, NA_HEADS, TILE, nk), F32),
        compiler_params=_cparams(("arbitrary", "arbitrary")),
        name="na_bias_table",
    )(col)


def _sw_kernel(q_ref, k_ref, v_ref, mask_ref, sink_ref, o_ref, *, t_off, t_lat):
    t = pl.program_id(1) + t_off
    lane = lax.broadcasted_iota(I32, (1, LANES), 1)
    low = lane < HEAD_DIM
    zero = jnp.zeros((SW_BLOCK, LANES), BF16)
    n_ctx_tiles = CTX_LEN // SW_BLOCK

    for hk in range(SW_KV_HEADS):
        kv_lanes = slice(hk * LANES, (hk + 1) * LANES)
        q_off = hk * 2 * LANES
        q01 = q_ref[:, q_off:q_off + LANES]
        q23 = q_ref[:, q_off + LANES:q_off + 2 * LANES]
        qs = jnp.concatenate([jnp.where(low, q01, zero), jnp.where(low, zero, q01),
                              jnp.where(low, q23, zero), jnp.where(low, zero, q23)], axis=0)
        kc = k_ref[0:CTX_LEN, kv_lanes]
        vc = v_ref[0:CTX_LEN, kv_lanes]
        sink = sink_ref[hk]
        s_cx = _dot_t(qs, kc)

        def finish(blocks, vals, sink=sink, q_off=q_off):
            m = sink
            for s in blocks:
                m = jnp.maximum(m, s.max(axis=-1, keepdims=True))
            ps = [jnp.exp(s - m) for s in blocks]
            l = jnp.exp(sink - m)
            for p in ps:
                l = l + p.sum(axis=-1, keepdims=True)
            o = _dot(ps[0].astype(BF16), vals[0])
            for p, v in zip(ps[1:], vals[1:]):
                o = o + _dot(p.astype(BF16), v)
            o = o / l
            b = SW_BLOCK
            o_ref[:, q_off:q_off + LANES] = jnp.where(low, o[0:b], o[b:2 * b]).astype(BF16)
            o_ref[:, q_off + LANES:q_off + 2 * LANES] = jnp.where(low, o[2 * b:3 * b], o[3 * b:4 * b]).astype(BF16)

        if t_off == 0:
            @pl.when(t < n_ctx_tiles)
            def _ctx():
                finish([s_cx], [vc])

        @pl.when(t >= n_ctx_tiles)
        def _lat():
            nk = 3 * SW_BLOCK
            start = jnp.clip((t - n_ctx_tiles - 1) * SW_BLOCK, 0, t_lat - nk)
            start = pl.multiple_of(start + CTX_LEN, SW_BLOCK)
            k = k_ref[pl.ds(start, nk), kv_lanes]
            v = v_ref[pl.ds(start, nk), kv_lanes]
            finish([_dot_t(qs, k) + mask_ref[...], s_cx], [v, vc])


def _sw_attention(qs, ks, vs, mask, sink_col, with_ctx):
    bsz, s, _ = qs.shape
    t_lat = s - CTX_LEN
    nb = t_lat // SW_BLOCK
    n_ctx_tiles = CTX_LEN // SW_BLOCK
    t_off = 0 if with_ctx else n_ctx_tiles
    steps = nb + n_ctx_tiles - t_off

    def var(i):
        n = jnp.maximum(i + t_off - n_ctx_tiles, 0)
        return jnp.where(n == 0, 0, jnp.where(n == nb - 1, 2, 1))

    g = SW_HEADS // SW_KV_HEADS
    return pl.pallas_call(
        functools.partial(_sw_kernel, t_off=t_off, t_lat=t_lat),
        grid=(bsz, steps),
        in_specs=[pl.BlockSpec((None, SW_BLOCK, BRANCH_W), lambda b, i: (b, i + t_off, 0)),
                  pl.BlockSpec((None, s, SW_KV_HEADS * LANES), lambda b, i: (b, 0, 0)),
                  pl.BlockSpec((None, s, SW_KV_HEADS * LANES), lambda b, i: (b, 0, 0)),
                  pl.BlockSpec((None, g * SW_BLOCK, 3 * SW_BLOCK), lambda b, i: (var(i), 0, 0)),
                  pl.BlockSpec((SW_KV_HEADS, g * SW_BLOCK, 1), lambda b, i: (0, 0, 0))],
        out_specs=pl.BlockSpec((None, SW_BLOCK, BRANCH_W), lambda b, i: (b, i, 0)),
        out_shape=jax.ShapeDtypeStruct((bsz, steps * SW_BLOCK, BRANCH_W), BF16),
        compiler_params=_cparams(("arbitrary", "arbitrary")),
        name="sw_attention",
    )(qs, ks, vs, mask, sink_col)


def _sw_mask_table():
    i = np.arange(SW_BLOCK)[:, None]
    j = np.arange(3 * SW_BLOCK)[None, :]
    tabs = []
    for shift in (0, SW_BLOCK, 2 * SW_BLOCK):
        rel = j - i - shift
        tabs.append(np.where(np.abs(rel) <= SW_WINDOW, 0.0, NEG_INF).astype(np.float32))
    tab = np.stack(tabs)
    return jnp.asarray(np.tile(tab, (1, SW_HEADS // SW_KV_HEADS, 1)))


def _lru_kernel(x_ref, xp_ref, xn_ref, cw_ref, cb_ref, wg_ref, bg_ref, c8_ref, o_ref,
                xs, carry, *, nc):
    d, i = pl.program_id(1), pl.program_id(2)
    ci = jnp.where(d == 0, i, jnp.where(i == 0, 0, nc - i))
    tc = TILE
    w = LRU_WIDTH

    no_prev = (ci == 0) | (ci == 1)
    no_next = (ci == 0) | (ci == nc - 1)
    xs[0:8, :] = jnp.where(no_prev, 0.0, xp_ref[...])
    xs[8:8 + tc, :] = x_ref[...]
    xs[8 + tc:16 + tc, :] = jnp.where(no_next, 0.0, xn_ref[...])
    u = cb_ref[...] + cw_ref[0:1, :] * xs[7:7 + tc, :]
    for j in range(1, LRU_CONV):
        u = u + cw_ref[j:j + 1, :] * xs[7 + j:7 + j + tc, :]

    g = _dot(u.astype(BF16), wg_ref[...]) + bg_ref[...]
    r = jax.nn.sigmoid(g[:, :w])
    ig = jax.nn.sigmoid(g[:, w:])
    log_a = c8_ref[...] * r
    a = jnp.exp(log_a)
    th = jnp.tanh(log_a)
    bb = jnp.sqrt(-2.0 * th / (1.0 - th)) * (ig * u)
    @pl.when(i == 0)
    def _init():
        carry[...] = jnp.zeros_like(carry)

    ng = tc // 8
    a3 = a.reshape(ng, 8, w)
    b3 = bb.reshape(ng, 8, w)
    sub = lax.broadcasted_iota(I32, (ng, 8, w), 1)

    def scan(reverse):
        ag, bg_ = a3, b3
        for s in (1, 2, 4):
            shift = 8 - s if reverse else s
            keep = (sub < 8 - s) if reverse else (sub >= s)
            a_sh = pltpu.roll(ag, shift, 1)
            b_sh = pltpu.roll(bg_, shift, 1)
            bg_ = jnp.where(keep, ag * b_sh + bg_, bg_)
            ag = jnp.where(keep, ag * a_sh, ag)
        h = carry[0:1, :]
        for gi in (range(ng - 1, -1, -1) if reverse else range(ng)):
            hg = bg_[gi] + ag[gi] * h
            o_ref[gi * 8:(gi + 1) * 8, :] = hg
            h = hg[0:1, :] if reverse else hg[7:8, :]
        carry[0:1, :] = h

    @pl.when(d == 0)
    def _fwd():
        scan(False)

    @pl.when(d == 1)
    def _bwd():
        scan(True)


def _lru(xb, conv_w, conv_b, wg, bg, c8):
    bsz, s, w = xb.shape
    nc = s // TILE
    r8 = TILE // 8

    def chunk(d, i):
        return jnp.where(d == 0, i, jnp.where(i == 0, 0, nc - i))

    return pl.pallas_call(
        functools.partial(_lru_kernel, nc=nc),
        grid=(bsz, 2, nc),
        in_specs=[pl.BlockSpec((None, TILE, w), lambda b, d, i: (b, chunk(d, i), 0)),
                  pl.BlockSpec((None, 8, w), lambda b, d, i: (b, jnp.maximum(chunk(d, i) * r8 - 1, 0), 0)),
                  pl.BlockSpec((None, 8, w), lambda b, d, i: (b, jnp.minimum((chunk(d, i) + 1) * r8, s // 8 - 1), 0)),
                  pl.BlockSpec((LRU_CONV, w), lambda b, d, i: (0, 0)),
                  pl.BlockSpec((1, w), lambda b, d, i: (0, 0)),
                  pl.BlockSpec((None, w, 2 * w), lambda b, d, i: (d, 0, 0)),
                  pl.BlockSpec((None, 1, 2 * w), lambda b, d, i: (d, 0, 0)),
                  pl.BlockSpec((None, 1, w), lambda b, d, i: (d, 0, 0))],
        out_specs=pl.BlockSpec((None, None, TILE, w), lambda b, d, i: (d, b, chunk(d, i), 0)),
        out_shape=jax.ShapeDtypeStruct((2, bsz, s, w), F32),
        scratch_shapes=[pltpu.VMEM((TILE + 16, w), F32), pltpu.VMEM((8, w), F32)],
        compiler_params=_cparams(("arbitrary", "arbitrary", "arbitrary")),
        name="rglru",
    )(xb, xb, xb, conv_w, conv_b, wg, bg, c8)


def _gelu_tanh(x):
    return 0.5 * x * (1.0 + jnp.tanh(np.sqrt(2.0 / np.pi).astype(np.float32) * (x + 0.044715 * (x * x * x))))


def _merge_kernel(xc_ref, xl_ref, ya_ref, hf_ref, hb_ref, gb_ref, yc_ref, sg_ref, mod_ref, gpost_ref, gpre_ref,
                  wbr_ref, wout_ref, wr_ref, br_ref, cnt0_ref,
                  x1_ref, h2_ref, idx_ref, wgt_ref, rank_ref, cnt_ref, cnt_s, *, t_off):
    b, ti = pl.program_id(0), pl.program_id(1)
    t = ti + t_off
    d = D_MODEL

    @pl.when((b == 0) & (ti == 0))
    def _init():
        cnt_s[...] = cnt0_ref[...]

    yb = (_gelu_tanh(gb_ref[...]) * (hf_ref[...] + hb_ref[...])).astype(BF16)
    ys = (ya_ref[...], yb, yc_ref[...])
    mix = sg_ref[:, 0:d] * _dot(ys[0], wbr_ref[0])
    for k in range(1, N_BRANCH):
        mix = mix + sg_ref[:, k * d:(k + 1) * d] * _dot(ys[k], wbr_ref[k])
    y = _dot(mix.astype(BF16), wout_ref[...])
    x = jnp.where(t == 0, xc_ref[...], xl_ref[...])
    x1 = x + _mod_row(mod_ref, b, t, 2) * _rms(y, gpost_ref[...])
    x1_ref[...] = x1
    h2 = _rms(x1, gpre_ref[...]) * (1.0 + _mod_row(mod_ref, b, t, 4)) + _mod_row(mod_ref, b, t, 3)
    _store_token_rows(h2_ref, h2, TILE)

    logits = _dot(h2.astype(BF16), wr_ref[...]) + br_ref[...]
    lane = lax.broadcasted_iota(I32, (TILE, LANES), 1)
    work = logits
    tops, sels = [], []
    for k in range(TOP_K):
        m = work.max(axis=-1, keepdims=True)
        first = jnp.where(work == m, lane, LANES).min(axis=-1, keepdims=True)
        sel = lane == first
        work = jnp.where(sel, -3.0e38, work)
        tops.append(m)
        sels.append(sel)
    es = [jnp.exp(m - tops[0]) for m in tops]
    den = es[0] + es[1] + es[2] + es[3]

    chosen = jnp.zeros((TILE, LANES), F32)
    for sel in sels:
        chosen = jnp.where(sel, 1.0, chosen)
    rr = lax.broadcasted_iota(I32, (TILE, TILE), 0)
    cc = lax.broadcasted_iota(I32, (TILE, TILE), 1)
    tri = jnp.where(cc < rr, 1.0, 0.0).astype(BF16)
    pref = _dot(tri, chosen.astype(BF16)) + cnt_s[0:1, :]
    cnt_new = cnt_s[0:1, :] + chosen.sum(axis=0, keepdims=True)
    cnt_s[0:1, :] = cnt_new
    cnt_ref[...] = jnp.broadcast_to(cnt_new, cnt_ref.shape)

    idx_o = jnp.zeros((TILE, LANES), I32)
    wgt_o = jnp.zeros((TILE, LANES), F32)
    rank_o = jnp.zeros((TILE, LANES), I32)
    for k in range(TOP_K):
        first = jnp.where(sels[k], lane, 0).max(axis=-1, keepdims=True)
        rk = jnp.where(sels[k], pref, 0.0).sum(axis=-1, keepdims=True)
        idx_o = jnp.where(lane == k, first, idx_o)
        wgt_o = jnp.where(lane == k, es[k] / den, wgt_o)
        rank_o = jnp.where(lane == k, rk.astype(I32), rank_o)
    idx_ref[...] = idx_o
    wgt_ref[...] = wgt_o
    rank_ref[...] = rank_o


def _merge(seq, ya, hfb, gb, yc, sg, mod, g_post, g_pre, w_br, w_out, w_r, b_r, cnt0, with_ctx):
    bsz, d = seq[0].shape[0], seq[0].shape[-1]
    s = _seq_len(seq)
    t_off = 0 if with_ctx else 1
    nt = s // TILE - t_off
    rows = nt * TILE
    inp = lambda b, t: (b, t + t_off, 0)
    loc = lambda b, t: (b, t, 0)
    c2 = lambda b, t: (0, 0)
    c3 = lambda b, t: (0, 0, 0)

    def out(width, dtype):
        return jax.ShapeDtypeStruct((bsz, rows, width), dtype), pl.BlockSpec((None, TILE, width), loc)

    h2_out = (jax.ShapeDtypeStruct((bsz, rows * TOK_ROWS, LANES), F32),
              pl.BlockSpec((None, TILE * TOK_ROWS, LANES), loc))
    outs = [out(d, F32), h2_out, out(LANES, I32), out(LANES, F32), out(LANES, I32)]
    shapes = [o[0] for o in outs] + [jax.ShapeDtypeStruct((8, LANES), F32)]
    specs = [o[1] for o in outs] + [pl.BlockSpec((8, LANES), c2)]
    return pl.pallas_call(
        functools.partial(_merge_kernel, t_off=t_off),
        grid=(bsz, nt),
        in_specs=_seq_specs(seq, t_off) + [
                  pl.BlockSpec((None, TILE, BRANCH_W), loc),
                  pl.BlockSpec((None, None, TILE, BRANCH_W), lambda b, t: (0, b, t + t_off, 0)),
                  pl.BlockSpec((None, None, TILE, BRANCH_W), lambda b, t: (1, b, t + t_off, 0)),
                  pl.BlockSpec((None, TILE, BRANCH_W), inp),
                  pl.BlockSpec((None, TILE, BRANCH_W), loc),
                  pl.BlockSpec((None, TILE, N_BRANCH * d), inp),
                  pl.BlockSpec((8, 6 * d), c2),
                  pl.BlockSpec((1, d), c2),
                  pl.BlockSpec((1, d), c2),
                  pl.BlockSpec((N_BRANCH, BRANCH_W, d), c3),
                  pl.BlockSpec((d, d), c2),
                  pl.BlockSpec((d, LANES), c2),
                  pl.BlockSpec((1, LANES), c2),
                  pl.BlockSpec((8, LANES), c2)],
        out_specs=specs,
        out_shape=shapes,
        scratch_shapes=[pltpu.VMEM((8, LANES), F32)],
        compiler_params=_cparams(("arbitrary", "arbitrary")),
        name="merge_router",
    )(seq[0], seq[1], ya, hfb, hfb, gb, yc, sg, mod, g_post, g_pre, w_br, w_out, w_r, b_r, cnt0)


def _store_token_rows(ref, val, n, lead=()):
    for j in range(TOK_ROWS):
        ref[lead + (pl.ds(j, n, stride=TOK_ROWS), slice(None))] = val[:, j * LANES:(j + 1) * LANES]


def _load_token_rows(ref, n, lead=()):
    return jnp.concatenate([ref[lead + (pl.ds(j, n, stride=TOK_ROWS), slice(None))] for j in range(TOK_ROWS)], axis=1)


def _token_slice(i):
    return pl.ds(pl.multiple_of(i * TOK_ROWS, TOK_ROWS), TOK_ROWS)


def _dispatch_kernel(pos_ref, last_ref, h_ref, xs_ref, zbuf, sem, zsem):
    step = pl.program_id(0)

    @pl.when(step == 0)
    def _zero_fill():
        zbuf[...] = jnp.zeros_like(zbuf)

        def fill(e):
            dst = xs_ref.at[pl.ds(pl.multiple_of(last_ref[0, e] * TOK_ROWS, TOK_ROWS), MOE_ROWS * TOK_ROWS)]
            return pltpu.make_async_copy(zbuf, dst, zsem)

        for e in range(N_EXPERTS):
            @pl.when(last_ref[0, e] >= 0)
            def _():
                fill(e).start()
        for e in range(N_EXPERTS):
            @pl.when(last_ref[0, e] >= 0)
            def _():
                fill(e).wait()

    def row_copy(i, k):
        p = pos_ref[0, 0, i * TOP_K + k]
        return pltpu.make_async_copy(h_ref.at[_token_slice(i)], xs_ref.at[_token_slice(p)], sem)

    def start(i, c):
        for k in range(TOP_K):
            row_copy(i, k).start(priority=k % 2)
        return c

    lax.fori_loop(0, TILE, start, 0, unroll=DMA_UNROLL)
    for _ in range(TOP_K):
        pltpu.make_async_copy(h_ref, xs_ref.at[pl.ds(0, TILE * TOK_ROWS)], sem).wait()


def _dispatch(h2, pos, last_tile_row, n_rows):
    bsz, rows = pos.shape[:2]
    n_steps = bsz * rows // TILE
    return pl.pallas_call(
        _dispatch_kernel,
        grid=(n_steps,),
        in_specs=[pl.BlockSpec((1, 1, TILE * TOP_K), lambda i: (i, 0, 0), memory_space=pltpu.SMEM),
                  pl.BlockSpec((1, N_EXPERTS), lambda i: (0, 0), memory_space=pltpu.SMEM),
                  pl.BlockSpec((TILE * TOK_ROWS, LANES), lambda i: (i, 0))],
        out_specs=pl.BlockSpec(memory_space=pl.ANY),
        out_shape=jax.ShapeDtypeStruct((n_rows * TOK_ROWS, LANES), F32),
        scratch_shapes=[pltpu.VMEM((MOE_ROWS * TOK_ROWS, LANES), F32), pltpu.SemaphoreType.DMA(()),
                        pltpu.SemaphoreType.DMA(())],
        compiler_params=_cparams(("arbitrary",)),
        name="moe_dispatch",
    )(pos.reshape(n_steps, 1, TILE * TOP_K), last_tile_row, h2.reshape(bsz * rows * TOK_ROWS, LANES))


def _expert_kernel(te_ref, nu_ref, x_ref, w1_ref, b1_ref, w2_ref, b2_ref, y_ref, w1_s, w2_s):
    t = pl.program_id(0)
    prev = te_ref[jnp.maximum(t - 1, 0)]
    active = t < nu_ref[0]

    @pl.when(active & ((t == 0) | (te_ref[t] != prev)))
    def _load():
        w1_s[...] = w1_ref[...].astype(BF16)
        w2_s[...] = w2_ref[...].astype(BF16)

    @pl.when(active)
    def _run():
        f = D_EXPERT
        x = _load_token_rows(x_ref, MOE_ROWS).astype(BF16)
        hid = _dot(x, w1_s[...]) + b1_ref[...]
        glu = jnp.minimum(hid[:, :f], SWIGLU_LIMIT)
        lin = jnp.clip(hid[:, f:], -SWIGLU_LIMIT, SWIGLU_LIMIT)
        act = glu * jax.nn.sigmoid(SWIGLU_ALPHA * glu) * (lin + 1.0)
        _store_token_rows(y_ref, _dot(act.astype(BF16), w2_s[...]) + b2_ref[...], MOE_ROWS)

    @pl.when(jnp.logical_not(active))
    def _idle():
        y_ref[...] = jnp.zeros_like(y_ref)


def _experts(xs, tile_expert, n_used, w1, b1, w2, b2, layer):
    depth, e, d, f2 = w1.shape
    n_tiles = xs.shape[0] // (MOE_ROWS * TOK_ROWS)
    blk = (MOE_ROWS * TOK_ROWS, LANES)
    xmap = lambda t, te, nu: (jnp.minimum(t, nu[0] - 1), 0)
    wmap = lambda t, te, nu: (layer, te[t], 0, 0)
    grid_spec = pltpu.PrefetchScalarGridSpec(
        num_scalar_prefetch=2,
        grid=(n_tiles,),
        in_specs=[pl.BlockSpec(blk, xmap),
                  pl.BlockSpec((None, None, d, f2), wmap),
                  pl.BlockSpec((None, None, 1, f2), wmap),
                  pl.BlockSpec((None, None, f2 // 2, d), wmap),
                  pl.BlockSpec((None, None, 1, d), wmap)],
        out_specs=pl.BlockSpec(blk, lambda t, te, nu: (t, 0)),
        scratch_shapes=[pltpu.VMEM((d, f2), BF16), pltpu.VMEM((f2 // 2, d), BF16)])
    return pl.pallas_call(
        _expert_kernel,
        grid_spec=grid_spec,
        out_shape=jax.ShapeDtypeStruct(xs.shape, F32),
        compiler_params=_cparams(("arbitrary",)),
        name="moe_experts",
    )(tile_expert, n_used, xs, w1, b1.reshape(depth, e, 1, f2), w2, b2.reshape(depth, e, 1, d))


def _combine_kernel(pos_ref, posn_ref, ys_ref, wgt_ref, x1_ref, mod_ref, gpost_ref, o_ref, buf, sem, *, t_off, nt):
    b, ti = pl.program_id(0), pl.program_id(1)
    t = ti + t_off
    step = b * nt + ti
    n_steps = pl.num_programs(0) * nt
    slot = step % 2

    def issue(p_ref, sl):
        def start(i, c):
            for k in range(TOP_K):
                p = p_ref[0, 0, i * TOP_K + k]
                pltpu.make_async_copy(ys_ref.at[_token_slice(p)], buf.at[sl, k, _token_slice(i)],
                                      sem.at[sl]).start(priority=k % 2)
            return c
        lax.fori_loop(0, TILE, start, 0, unroll=DMA_UNROLL)

    @pl.when(step == 0)
    def _():
        issue(pos_ref, 0)

    @pl.when(step + 1 < n_steps)
    def _():
        issue(posn_ref, 1 - slot)

    for k in range(TOP_K):
        pltpu.make_async_copy(ys_ref.at[pl.ds(0, TILE * TOK_ROWS)], buf.at[slot, k], sem.at[slot]).wait()

    wgt = wgt_ref[...]
    moe = wgt[:, 0:1] * _load_token_rows(buf, TILE, (slot, 0))
    for k in range(1, TOP_K):
        moe = moe + wgt[:, k:k + 1] * _load_token_rows(buf, TILE, (slot, k))
    o_ref[...] = x1_ref[...] + _mod_row(mod_ref, b, t, 5) * _rms(moe, gpost_ref[...])


def _combine(ys, pos, wgt, x1, mod, g_post, with_ctx):
    bsz, rows, d = x1.shape
    nt = rows // TILE
    tok = lambda b, t: (b, t, 0)
    last = bsz * nt - 1
    return pl.pallas_call(
        functools.partial(_combine_kernel, t_off=0 if with_ctx else 1, nt=nt),
        grid=(bsz, nt),
        in_specs=[pl.BlockSpec((1, 1, TILE * TOP_K), lambda b, t: (b * nt + t, 0, 0), memory_space=pltpu.SMEM),
                  pl.BlockSpec((1, 1, TILE * TOP_K), lambda b, t: (jnp.minimum(b * nt + t + 1, last), 0, 0),
                               memory_space=pltpu.SMEM),
                  pl.BlockSpec(memory_space=pl.ANY),
                  pl.BlockSpec((None, TILE, LANES), tok),
                  pl.BlockSpec((None, TILE, d), tok),
                  pl.BlockSpec((8, 6 * d), lambda b, t: (0, 0)),
                  pl.BlockSpec((1, d), lambda b, t: (0, 0))],
        out_specs=pl.BlockSpec((None, TILE, d), tok),
        out_shape=jax.ShapeDtypeStruct((bsz, rows, d), F32),
        scratch_shapes=[pltpu.VMEM((2, TOP_K, TILE * TOK_ROWS, LANES), F32), pltpu.SemaphoreType.DMA((2,))],
        compiler_params=_cparams(("arbitrary", "arbitrary")),
        name="moe_combine",
    )(pos.reshape(bsz * nt, 1, TILE * TOP_K), pos.reshape(bsz * nt, 1, TILE * TOP_K), ys, wgt, x1, mod, g_post)


def _moe(h2, idx, wgt, rank, counts, x1, mod, g_post, w1, b1, w2, b2, layer, with_ctx):
    bsz, rows, d = x1.shape
    n_tiles = (bsz * rows * TOP_K) // MOE_ROWS + N_EXPERTS
    n_rows = n_tiles * MOE_ROWS
    cnt = counts.astype(I32)
    tiles_e = (cnt + MOE_ROWS - 1) // MOE_ROWS
    tile_end = jnp.cumsum(tiles_e)
    row_start = (tile_end - tiles_e) * MOE_ROWS
    n_used = tile_end[-1:]
    tile_ids = jnp.minimum(jnp.arange(n_tiles, dtype=I32), n_used[0] - 1)
    tile_expert = jnp.sum((tile_end[None, :] <= tile_ids[:, None]).astype(I32), axis=1)
    tile_expert = jnp.minimum(tile_expert, N_EXPERTS - 1)
    experts = jnp.arange(N_EXPERTS, dtype=I32)
    start_of = jnp.sum(jnp.where(idx[..., :TOP_K, None] == experts, row_start, 0), axis=-1)
    pos = (start_of + rank[..., :TOP_K]).astype(I32)
    last_tile_row = jnp.where(tiles_e > 0, (tile_end - 1) * MOE_ROWS, -1).astype(I32)[None, :]
    xs = _dispatch(h2, pos, last_tile_row, n_rows)
    ys = _experts(xs, tile_expert, n_used.astype(I32), w1, b1, w2, b2, layer)
    return _combine(ys, pos, wgt, x1, mod, g_post, with_ctx)


def _dup_heads(w):
    d = w.shape[0]
    w = w.reshape(d, -1, 1, HEAD_DIM)
    return jnp.broadcast_to(w, (d, w.shape[1], 2, HEAD_DIM)).reshape(d, -1)


def _inproj_weight(w_in):
    sizes = (BRANCH_W, BRANCH_W, BRANCH_W, LRU_WIDTH, LRU_WIDTH, BRANCH_W,
             SW_KV_HEADS * HEAD_DIM, SW_KV_HEADS * HEAD_DIM, N_BRANCH * D_MODEL)
    offs = np.cumsum((0,) + sizes)
    qa, ka, va, gb, xb, qs, ks, vs, gl = [w_in[:, offs[i]:offs[i + 1]] for i in range(9)]
    scale = HEAD_DIM ** -0.5
    cols = [qa * scale, ka, va, gb, xb, qs * scale, _dup_heads(ks), _dup_heads(vs), gl]
    return jnp.concatenate(cols, axis=1).astype(BF16)


def _rope_tables(t_lat):
    pos = np.arange(t_lat)
    row = (pos // GRID_W).astype(np.float32)
    col = (pos % GRID_W).astype(np.float32)
    half = ROPE_AXIS_DIM // 2
    inv = (ROPE_BASE ** (-jnp.arange(0, ROPE_AXIS_DIM, 2, dtype=F32) / ROPE_AXIS_DIM))
    d = np.arange(HEAD_DIM)
    axis = d // ROPE_AXIS_DIM
    freq = d % half
    sign = np.where((d % ROPE_AXIS_DIM) < half, -1.0, 1.0).astype(np.float32)
    p = jnp.where(jnp.asarray(axis == 0)[None, :], jnp.asarray(row)[:, None], jnp.asarray(col)[:, None])
    ang = p * inv[freq][None, :]
    cos = jnp.concatenate([jnp.ones((CTX_LEN, HEAD_DIM), F32), jnp.cos(ang)], axis=0)
    sin = jnp.concatenate([jnp.zeros((CTX_LEN, HEAD_DIM), F32), jnp.sin(ang) * sign[None, :]], axis=0)
    return jnp.tile(cos, (1, 2)), jnp.tile(sin, (1, 2))


def _lru_gate_weights(wa, ba, wx, bx):
    def dense(wblk):
        k, bs = wblk.shape[1], wblk.shape[2]
        eye = jnp.eye(k, dtype=wblk.dtype)
        return jnp.einsum("dkij,kl->dkilj", wblk, eye).reshape(2, k * bs, k * bs)
    wg = jnp.concatenate([dense(wa), dense(wx)], axis=-1).astype(BF16)
    bg = jnp.concatenate([ba, bx], axis=-1)[:, None, :].astype(F32)
    return wg, bg


def _layer(seq, mod, g_mix_pre, g_mix_post, g_ffn_pre, g_ffn_post, w_in, na_rpb, conv_w, conv_b,
           lru_wa, lru_ba, lru_wx, lru_bx, lru_lam, sw_sinks, w_branch, w_out, w_router, b_router,
           w1, b1, w2, b2, layer, with_ctx):
    d = seq[0].shape[-1]
    t_lat = _seq_len(seq) - CTX_LEN
    row = lambda v: v.reshape(1, -1).astype(F32)

    cos_t, sin_t = _rope_tables(t_lat)
    qa, ka, va, gb, xb, qs, ks, vs, sg = _inproj(seq, mod, row(g_mix_pre), _inproj_weight(w_in), cos_t, sin_t)

    ya = _na_attention(qa, ka, va, _na_bias_table(na_rpb, t_lat // GRID_W), with_ctx)

    wg, bg = _lru_gate_weights(lru_wa, lru_ba, lru_wx, lru_bx)
    c8 = (-LRU_C * jax.nn.softplus(-lru_lam.astype(F32)))[:, None, :]
    hfb = _lru(xb, conv_w.astype(F32), row(conv_b), wg, bg, c8)

    g = SW_HEADS // SW_KV_HEADS
    sink_col = jnp.repeat(sw_sinks.astype(F32).reshape(SW_KV_HEADS, g), SW_BLOCK, axis=1)[:, :, None]
    yc = _sw_attention(qs, ks, vs, _sw_mask_table(), sink_col, with_ctx)

    w_r = jnp.zeros((d, LANES), F32).at[:, :N_EXPERTS].set(w_router).astype(BF16)
    b_r = jnp.full((1, LANES), NEG_INF, F32).at[0, :N_EXPERTS].set(b_router.astype(F32))
    x1, h2, idx, wgt, rank, cnt = _merge(
        seq, ya, hfb, gb, yc, sg, mod, row(g_mix_post), row(g_ffn_pre), w_branch.astype(BF16),
        w_out.astype(BF16), w_r, b_r, jnp.zeros((8, LANES), F32), with_ctx)

    return _moe(h2, idx, wgt, rank, cnt[0, :N_EXPERTS], x1, mod, row(g_ffn_post), w1, b1, w2, b2, layer, with_ctx)


def kernel(x, c, ctx, c_ctx, w_ada, b_ada, g_mix_pre, g_mix_post, g_ffn_pre, g_ffn_post, w_in, na_rpb, conv_w, conv_b, lru_wa, lru_ba, lru_wx, lru_bx, lru_lam, sw_sinks, w_branch, w_out, w_router, b_router, w1, b1, w2, b2):
    depth = w_ada.shape[0]
    bsz, t_lat, d = x.shape
    assert bsz == 2 and d == D_MODEL and ctx.shape[1] == CTX_LEN
    assert t_lat % TILE == 0 and t_lat // GRID_W >= NA_KROWS and t_lat // SW_BLOCK >= 3
    c8 = jnp.zeros((8, d), F32).at[:bsz].set(c).at[bsz].set(c_ctx)
    mods = _ada(c8, w_ada, b_ada)
    seq = (ctx, x, 0)
    for l in range(depth):
        with_ctx = l < depth - 1
        out = _layer(seq, mods[l], g_mix_pre[l], g_mix_post[l], g_ffn_pre[l], g_ffn_post[l], w_in[l], na_rpb[l],
                     conv_w[l], conv_b[l], lru_wa[l], lru_ba[l], lru_wx[l], lru_bx[l], lru_lam[l], sw_sinks[l],
                     w_branch[l], w_out[l], w_router[l], b_router[l], w1, b1, w2, b2, l, with_ctx)
        seq = (out, out, 1)
    return out
```

```python
import functools

import numpy as np
import jax
import jax.numpy as jnp
from jax import lax
from jax.experimental import pallas as pl
from jax.experimental.pallas import tpu as pltpu

F32 = jnp.float32
BF16 = jnp.bfloat16
I32 = jnp.int32

D_MODEL = 1024
CTX_LEN = 256
GRID_W = 64
HEAD_DIM = 64
BRANCH_W = 512
N_BRANCH = 3
NA_HEADS = 8
NA_WIN_R = 8
NA_WIN_C = 16
LRU_WIDTH = 512
LRU_BLOCKS = 8
LRU_CONV = 4
LRU_C = 8.0
SW_HEADS = 8
SW_KV_HEADS = 2
SW_WINDOW = 128
SW_BLOCK = 128
ROPE_BASE = 10000.0
ROPE_AXIS_DIM = HEAD_DIM // 2
N_EXPERTS = 32
TOP_K = 4
D_EXPERT = 1024
SWIGLU_LIMIT = 7.0
SWIGLU_ALPHA = 1.702
RMS_EPS = 1e-6
NEG_INF = -1e30

LANES = 128
TILE = 256
NA_QROWS = 4
NA_KROWS = 12
NA_PAIRS = 2
MOE_ROWS = 512
DISPATCH_TILE = 512
TOK_ROWS = D_MODEL // LANES
DMA_UNROLL = 8
VMEM_LIMIT = 56 * 1024 * 1024

_C_QA, _C_KA, _C_VA, _C_GB, _C_XB, _C_QS, _C_KS, _C_VS, _C_GL, _C_END = (
    0, 512, 1024, 1536, 2048, 2560, 3072, 3328, 3584, 6656)


def _cparams(sem, vmem=VMEM_LIMIT):
    return pltpu.CompilerParams(dimension_semantics=sem, vmem_limit_bytes=vmem)


def _dot(a, b):
    return jnp.dot(a, b, preferred_element_type=F32)


def _dot_t(a, b):
    return lax.dot_general(a, b, (((1,), (1,)), ((), ())), preferred_element_type=F32)


def _rms(x, g):
    return x * lax.rsqrt(jnp.mean(x * x, axis=-1, keepdims=True) + RMS_EPS) * g


def _ada_kernel(c_ref, w_ref, b_ref, o_ref):
    c = c_ref[...]
    s = c * jax.nn.sigmoid(c)
    o_ref[...] = jnp.dot(s, w_ref[...], preferred_element_type=F32, precision=lax.Precision.HIGHEST) + b_ref[...]


def _ada(c8, w_ada, b_ada):
    depth, d, n = w_ada.shape
    tn = 1536
    return pl.pallas_call(
        _ada_kernel,
        grid=(depth, n // tn),
        in_specs=[pl.BlockSpec((8, d), lambda l, j: (0, 0)),
                  pl.BlockSpec((None, d, tn), lambda l, j: (l, 0, j)),
                  pl.BlockSpec((None, 1, tn), lambda l, j: (l, 0, j))],
        out_specs=pl.BlockSpec((None, 8, tn), lambda l, j: (l, 0, j)),
        out_shape=jax.ShapeDtypeStruct((depth, 8, n), F32),
        compiler_params=_cparams(("arbitrary", "arbitrary")),
        name="ada_mod",
    )(c8, w_ada, b_ada.reshape(depth, 1, n))


def _mod_row(mod_ref, b, t, k):
    row = jnp.where(t == 0, 2, b)
    return mod_ref[pl.ds(row, 1), k * D_MODEL:(k + 1) * D_MODEL]


def _inproj_kernel(xc_ref, xl_ref, mod_ref, g_ref, w_ref, cos_ref, sin_ref,
                   qa_ref, ka_ref, va_ref, gb_ref, xb_ref, qs_ref, ks_ref, vs_ref, sg_ref):
    b, t = pl.program_id(0), pl.program_id(1)
    x = jnp.where(t == 0, xc_ref[...], xl_ref[...])
    h = _rms(x, g_ref[...]) * (1.0 + _mod_row(mod_ref, b, t, 1)) + _mod_row(mod_ref, b, t, 0)
    h = h.astype(BF16)

    def seg(lo, hi):
        return _dot(h, w_ref[:, lo:hi])

    qa_ref[...] = seg(_C_QA, _C_KA).astype(BF16)
    ka_ref[...] = seg(_C_KA, _C_VA).astype(BF16)
    va_ref[...] = seg(_C_VA, _C_GB).astype(BF16)
    gb_ref[...] = seg(_C_GB, _C_XB)
    xb_ref[...] = seg(_C_XB, _C_QS)
    cos = cos_ref[...]
    sin = sin_ref[...]
    half = ROPE_AXIS_DIM // 2
    lane = lax.broadcasted_iota(I32, (1, LANES), 1)
    first_half = (lane % ROPE_AXIS_DIM) < half

    def rope(v):
        partner = jnp.where(first_half, pltpu.roll(v, LANES - half, 1), pltpu.roll(v, half, 1))
        return (v * cos + partner * sin).astype(BF16)

    for j in range(4):
        qs_ref[:, j * LANES:(j + 1) * LANES] = rope(seg(_C_QS + j * LANES, _C_QS + (j + 1) * LANES))
    for j in range(2):
        ks_ref[:, j * LANES:(j + 1) * LANES] = rope(seg(_C_KS + j * LANES, _C_KS + (j + 1) * LANES))
    vs_ref[...] = seg(_C_VS, _C_GL).astype(BF16)
    for j in range(N_BRANCH):
        lo = _C_GL + j * D_MODEL
        sg_ref[:, j * D_MODEL:(j + 1) * D_MODEL] = jax.nn.sigmoid(seg(lo, lo + D_MODEL))


def _seq_specs(seq, t_off=0):
    xc, xl, lat_off = seq
    d = xc.shape[-1]
    return [pl.BlockSpec((None, TILE, d), lambda b, i: (b, 0, 0)),
            pl.BlockSpec((None, TILE, d), lambda b, i: (b, jnp.maximum(i + t_off - 1, 0) + lat_off, 0))]


def _seq_len(seq):
    return CTX_LEN + seq[1].shape[1] - seq[2] * TILE


def _inproj(seq, mod, g_pre, w_cat, cos_t, sin_t):
    bsz, d = seq[0].shape[0], seq[0].shape[-1]
    s = _seq_len(seq)
    nt = s // TILE
    tok = lambda b, t: (b, t, 0)
    const2 = lambda b, t: (0, 0)

    def out(width, dtype):
        return jax.ShapeDtypeStruct((bsz, s, width), dtype), pl.BlockSpec((None, TILE, width), tok)

    outs = [out(512, BF16), out(512, BF16), out(512, BF16), out(512, F32), out(512, F32),
            out(512, BF16), out(256, BF16), out(256, BF16), out(N_BRANCH * D_MODEL, F32)]
    return pl.pallas_call(
        _inproj_kernel,
        grid=(bsz, nt),
        in_specs=_seq_specs(seq) + [
                  pl.BlockSpec((8, 6 * d), const2),
                  pl.BlockSpec((1, d), const2),
                  pl.BlockSpec((d, _C_END), const2),
                  pl.BlockSpec((TILE, LANES), lambda b, t: (t, 0)),
                  pl.BlockSpec((TILE, LANES), lambda b, t: (t, 0))],
        out_specs=[o[1] for o in outs],
        out_shape=[o[0] for o in outs],
        compiler_params=_cparams(("arbitrary", "arbitrary")),
        name="inproj",
    )(seq[0], seq[1], mod, g_pre, w_cat, cos_t, sin_t)


def _two_head_attention(q, score_fn, value_fn):
    lane = lax.broadcasted_iota(I32, (1, LANES), 1)
    low = lane < HEAD_DIM
    outs = []
    for hh in range(2):
        qm = jnp.where(low if hh == 0 else jnp.logical_not(low), q, jnp.zeros_like(q))
        blocks = score_fn(qm, hh)
        m = blocks[0].max(axis=-1, keepdims=True)
        for s in blocks[1:]:
            m = jnp.maximum(m, s.max(axis=-1, keepdims=True))
        ps = [jnp.exp(s - m) for s in blocks]
        l = ps[0].sum(axis=-1, keepdims=True)
        for p in ps[1:]:
            l = l + p.sum(axis=-1, keepdims=True)
        o = value_fn([p.astype(BF16) for p in ps])
        outs.append(o / l)
    return jnp.where(low, outs[0], outs[1])


def _na_kernel(q_ref, k_ref, v_ref, tab_ref, o_ref, *, t_off, t_lat):
    t = pl.program_id(2) + t_off

    for pair in range(NA_PAIRS):
        lanes = slice(pair * LANES, (pair + 1) * LANES)
        q = q_ref[:, lanes]
        kc = k_ref[0:CTX_LEN, lanes]
        vc = v_ref[0:CTX_LEN, lanes]

        if t_off == 0:
            @pl.when(t == 0)
            def _ctx():
                o = _two_head_attention(q, lambda qm, hh: [_dot_t(qm, kc)], lambda ps: _dot(ps[0], vc))
                o_ref[:, lanes] = o.astype(BF16)

        @pl.when(t > 0)
        def _lat():
            nk = NA_KROWS * GRID_W
            start = jnp.clip((t - 2) * TILE, 0, t_lat - nk)
            start = pl.multiple_of(start + CTX_LEN, TILE)
            k = k_ref[pl.ds(start, nk), lanes]
            v = v_ref[pl.ds(start, nk), lanes]
            o = _two_head_attention(
                q,
                lambda qm, hh: [_dot_t(qm, k) + tab_ref[2 * pair + hh], _dot_t(qm, kc)],
                lambda ps: _dot(ps[0], v) + _dot(ps[1], vc))
            o_ref[:, lanes] = o.astype(BF16)


def _na_attention(qa, ka, va, table, with_ctx):
    bsz, s, _ = qa.shape
    t_lat = s - CTX_LEN
    nrb = t_lat // TILE
    t_off = 0 if with_ctx else 1
    steps = nrb + 1 - t_off
    out_rows = steps * TILE

    def var(i):
        rb = jnp.maximum(i + t_off - 1, 0)
        return jnp.where(rb == 0, 0, jnp.where(rb == nrb - 1, 2, 1))

    w = NA_PAIRS * LANES
    return pl.pallas_call(
        functools.partial(_na_kernel, t_off=t_off, t_lat=t_lat),
        grid=(bsz, NA_HEADS // (2 * NA_PAIRS), steps),
        in_specs=[pl.BlockSpec((None, TILE, w), lambda b, p, i: (b, i + t_off, p)),
                  pl.BlockSpec((None, s, w), lambda b, p, i: (b, 0, p)),
                  pl.BlockSpec((None, s, w), lambda b, p, i: (b, 0, p)),
                  pl.BlockSpec((None, 2 * NA_PAIRS, TILE, NA_KROWS * GRID_W), lambda b, p, i: (var(i), p, 0, 0))],
        out_specs=pl.BlockSpec((None, TILE, w), lambda b, p, i: (b, i, p)),
        out_shape=jax.ShapeDtypeStruct((bsz, out_rows, BRANCH_W), BF16),
        compiler_params=_cparams(("arbitrary", "arbitrary", "arbitrary")),
        name="na_attention",
    )(qa, ka, va, table)


def _na_row_geometry(rows):
    qr = np.arange(NA_QROWS)
    kr = np.arange(NA_KROWS)
    r0 = np.array([0, NA_WIN_R // 2, rows - NA_QROWS])
    kr0 = np.array([0, 0, rows - NA_KROWS])
    q_abs = r0[:, None] + qr[None, :]
    k_abs = kr0[:, None] + kr[None, :]
    rs = np.clip(q_abs - NA_WIN_R // 2, 0, rows - NA_WIN_R)
    rvalid = (k_abs[:, None, :] >= rs[:, :, None]) & (k_abs[:, None, :] < rs[:, :, None] + NA_WIN_R)
    ridx = np.clip(k_abs[:, None, :] - q_abs[:, :, None] + NA_WIN_R - 1, 0, 2 * NA_WIN_R - 2)
    return ridx, rvalid


def _na_table_kernel(col_ref, o_ref, *, ridx, rvalid):
    lane = lax.broadcasted_iota(I32, (GRID_W, LANES), 1)
    low = lane < GRID_W
    neg = jnp.full((GRID_W, LANES), NEG_INF, F32)
    for v in range(3):
        @pl.when(pl.program_id(0) == v)
        def _():
            for q in range(NA_QROWS):
                for m in range(NA_KROWS // 2):
                    halves = []
                    for k in (2 * m, 2 * m + 1):
                        halves.append(col_ref[int(ridx[v, q, k])] if rvalid[v, q, k] else neg)
                    o_ref[q * GRID_W:(q + 1) * GRID_W, m * LANES:(m + 1) * LANES] = jnp.where(low, halves[0], halves[1])


def _na_bias_table(rpb, rows):
    ridx, rvalid = _na_row_geometry(rows)
    qc = np.arange(GRID_W)
    kc = np.arange(GRID_W)
    q_start = np.clip(qc - NA_WIN_C // 2, 0, GRID_W - NA_WIN_C)
    cvalid = (kc[None, :] >= q_start[:, None]) & (kc[None, :] < q_start[:, None] + NA_WIN_C)
    cidx = np.clip(kc[None, :] - qc[:, None] + NA_WIN_C - 1, 0, 2 * NA_WIN_C - 2)
    nr, ncol = 2 * NA_WIN_R - 1, 2 * NA_WIN_C - 1
    csel = (np.where(cvalid, cidx, ncol)[..., None] == np.arange(ncol + 1)).astype(np.float32)
    rpb_ext = jnp.concatenate([rpb.astype(F32), jnp.full(rpb.shape[:2] + (1,), NEG_INF, F32)], axis=-1)
    col = jnp.einsum("hij,cdj->hicd", rpb_ext, jnp.asarray(csel), precision=lax.Precision.HIGHEST)
    col = jnp.concatenate([col, col], axis=-1)
    nk = NA_KROWS * GRID_W
    return pl.pallas_call(
        functools.partial(_na_table_kernel, ridx=ridx, rvalid=rvalid),
        grid=(3, NA_HEADS),
        in_specs=[pl.BlockSpec((None, nr, GRID_W, LANES), lambda v, h: (h, 0, 0, 0))],
        out_specs=pl.BlockSpec((None, None, TILE, nk), lambda v, h: (v, h, 0, 0)),
        out_shape=jax.ShapeDtypeStruct((3, NA_HEADS, TILE, nk), F32),
        compiler_params=_cparams(("arbitrary", "arbitrary")),
        name="na_bias_table",
    )(col)


def _sw_kernel(q_ref, k_ref, v_ref, mask_ref, sink_ref, o_ref, *, t_off, t_lat):
    t = pl.program_id(1) + t_off
    lane = lax.broadcasted_iota(I32, (1, LANES), 1)
    low = lane < HEAD_DIM
    zero = jnp.zeros((SW_BLOCK, LANES), BF16)
    n_ctx_tiles = CTX_LEN // SW_BLOCK

    for hk in range(SW_KV_HEADS):
        kv_lanes = slice(hk * LANES, (hk + 1) * LANES)
        q_off = hk * 2 * LANES
        q01 = q_ref[:, q_off:q_off + LANES]
        q23 = q_ref[:, q_off + LANES:q_off + 2 * LANES]
        qs = jnp.concatenate([jnp.where(low, q01, zero), jnp.where(low, zero, q01),
                              jnp.where(low, q23, zero), jnp.where(low, zero, q23)], axis=0)
        kc = k_ref[0:CTX_LEN, kv_lanes]
        vc = v_ref[0:CTX_LEN, kv_lanes]
        sink = sink_ref[hk]
        s_cx = _dot_t(qs, kc)

        def finish(blocks, vals, sink=sink, q_off=q_off):
            m = sink
            for s in blocks:
                m = jnp.maximum(m, s.max(axis=-1, keepdims=True))
            ps = [jnp.exp(s - m) for s in blocks]
            l = jnp.exp(sink - m)
            for p in ps:
                l = l + p.sum(axis=-1, keepdims=True)
            o = _dot(ps[0].astype(BF16), vals[0])
            for p, v in zip(ps[1:], vals[1:]):
                o = o + _dot(p.astype(BF16), v)
            o = o / l
            b = SW_BLOCK
            o_ref[:, q_off:q_off + LANES] = jnp.where(low, o[0:b], o[b:2 * b]).astype(BF16)
            o_ref[:, q_off + LANES:q_off + 2 * LANES] = jnp.where(low, o[2 * b:3 * b], o[3 * b:4 * b]).astype(BF16)

        if t_off == 0:
            @pl.when(t < n_ctx_tiles)
            def _ctx():
                finish([s_cx], [vc])

        @pl.when(t >= n_ctx_tiles)
        def _lat():
            nk = 3 * SW_BLOCK
            start = jnp.clip((t - n_ctx_tiles - 1) * SW_BLOCK, 0, t_lat - nk)
            start = pl.multiple_of(start + CTX_LEN, SW_BLOCK)
            k = k_ref[pl.ds(start, nk), kv_lanes]
            v = v_ref[pl.ds(start, nk), kv_lanes]
            finish([_dot_t(qs, k) + mask_ref[...], s_cx], [v, vc])


def _sw_attention(qs, ks, vs, mask, sink_col, with_ctx):
    bsz, s, _ = qs.shape
    t_lat = s - CTX_LEN
    nb = t_lat // SW_BLOCK
    n_ctx_tiles = CTX_LEN // SW_BLOCK
    t_off = 0 if with_ctx else n_ctx_tiles
    steps = nb + n_ctx_tiles - t_off

    def var(i):
        n = jnp.maximum(i + t_off - n_ctx_tiles, 0)
        return jnp.where(n == 0, 0, jnp.where(n == nb - 1, 2, 1))

    g = SW_HEADS // SW_KV_HEADS
    return pl.pallas_call(
        functools.partial(_sw_kernel, t_off=t_off, t_lat=t_lat),
        grid=(bsz, steps),
        in_specs=[pl.BlockSpec((None, SW_BLOCK, BRANCH_W), lambda b, i: (b, i + t_off, 0)),
                  pl.BlockSpec((None, s, SW_KV_HEADS * LANES), lambda b, i: (b, 0, 0)),
                  pl.BlockSpec((None, s, SW_KV_HEADS * LANES), lambda b, i: (b, 0, 0)),
                  pl.BlockSpec((None, g * SW_BLOCK, 3 * SW_BLOCK), lambda b, i: (var(i), 0, 0)),
                  pl.BlockSpec((SW_KV_HEADS, g * SW_BLOCK, 1), lambda b, i: (0, 0, 0))],
        out_specs=pl.BlockSpec((None, SW_BLOCK, BRANCH_W), lambda b, i: (b, i, 0)),
        out_shape=jax.ShapeDtypeStruct((bsz, steps * SW_BLOCK, BRANCH_W), BF16),
        compiler_params=_cparams(("arbitrary", "arbitrary")),
        name="sw_attention",
    )(qs, ks, vs, mask, sink_col)


def _sw_mask_table():
    i = np.arange(SW_BLOCK)[:, None]
    j = np.arange(3 * SW_BLOCK)[None, :]
    tabs = []
    for shift in (0, SW_BLOCK, 2 * SW_BLOCK):
        rel = j - i - shift
        tabs.append(np.where(np.abs(rel) <= SW_WINDOW, 0.0, NEG_INF).astype(np.float32))
    tab = np.stack(tabs)
    return jnp.asarray(np.tile(tab, (1, SW_HEADS // SW_KV_HEADS, 1)))


def _lru_kernel(x_ref, xp_ref, xn_ref, cw_ref, cb_ref, wg_ref, bg_ref, c8_ref, o_ref,
                xs, carry, *, nc):
    d, i = pl.program_id(1), pl.program_id(2)
    ci = jnp.where(d == 0, i, jnp.where(i == 0, 0, nc - i))
    tc = TILE
    w = LRU_WIDTH

    no_prev = (ci == 0) | (ci == 1)
    no_next = (ci == 0) | (ci == nc - 1)
    xs[0:8, :] = jnp.where(no_prev, 0.0, xp_ref[...])
    xs[8:8 + tc, :] = x_ref[...]
    xs[8 + tc:16 + tc, :] = jnp.where(no_next, 0.0, xn_ref[...])
    u = cb_ref[...] + cw_ref[0:1, :] * xs[7:7 + tc, :]
    for j in range(1, LRU_CONV):
        u = u + cw_ref[j:j + 1, :] * xs[7 + j:7 + j + tc, :]

    g = _dot(u.astype(BF16), wg_ref[...]) + bg_ref[...]
    r = jax.nn.sigmoid(g[:, :w])
    ig = jax.nn.sigmoid(g[:, w:])
    log_a = c8_ref[...] * r
    a = jnp.exp(log_a)
    th = jnp.tanh(log_a)
    bb = jnp.sqrt(-2.0 * th / (1.0 - th)) * (ig * u)
    @pl.when(i == 0)
    def _init():
        carry[...] = jnp.zeros_like(carry)

    ng = tc // 8
    a3 = a.reshape(ng, 8, w)
    b3 = bb.reshape(ng, 8, w)
    sub = lax.broadcasted_iota(I32, (ng, 8, w), 1)

    def scan(reverse):
        ag, bg_ = a3, b3
        for s in (1, 2, 4):
            shift = 8 - s if reverse else s
            keep = (sub < 8 - s) if reverse else (sub >= s)
            a_sh = pltpu.roll(ag, shift, 1)
            b_sh = pltpu.roll(bg_, shift, 1)
            bg_ = jnp.where(keep, ag * b_sh + bg_, bg_)
            ag = jnp.where(keep, ag * a_sh, ag)
        h = carry[0:1, :]
        for gi in (range(ng - 1, -1, -1) if reverse else range(ng)):
            hg = bg_[gi] + ag[gi] * h
            o_ref[gi * 8:(gi + 1) * 8, :] = hg
            h = hg[0:1, :] if reverse else hg[7:8, :]
        carry[0:1, :] = h

    @pl.when(d == 0)
    def _fwd():
        scan(False)

    @pl.when(d == 1)
    def _bwd():
        scan(True)


def _lru(xb, conv_w, conv_b, wg, bg, c8):
    bsz, s, w = xb.shape
    nc = s // TILE
    r8 = TILE // 8

    def chunk(d, i):
        return jnp.where(d == 0, i, jnp.where(i == 0, 0, nc - i))

    return pl.pallas_call(
        functools.partial(_lru_kernel, nc=nc),
        grid=(bsz, 2, nc),
        in_specs=[pl.BlockSpec((None, TILE, w), lambda b, d, i: (b, chunk(d, i), 0)),
                  pl.BlockSpec((None, 8, w), lambda b, d, i: (b, jnp.maximum(chunk(d, i) * r8 - 1, 0), 0)),
                  pl.BlockSpec((None, 8, w), lambda b, d, i: (b, jnp.minimum((chunk(d, i) + 1) * r8, s // 8 - 1), 0)),
                  pl.BlockSpec((LRU_CONV, w), lambda b, d, i: (0, 0)),
                  pl.BlockSpec((1, w), lambda b, d, i: (0, 0)),
                  pl.BlockSpec((None, w, 2 * w), lambda b, d, i: (d, 0, 0)),
                  pl.BlockSpec((None, 1, 2 * w), lambda b, d, i: (d, 0, 0)),
                  pl.BlockSpec((None, 1, w), lambda b, d, i: (d, 0, 0))],
        out_specs=pl.BlockSpec((None, None, TILE, w), lambda b, d, i: (d, b, chunk(d, i), 0)),
        out_shape=jax.ShapeDtypeStruct((2, bsz, s, w), F32),
        scratch_shapes=[pltpu.VMEM((TILE + 16, w), F32), pltpu.VMEM((8, w), F32)],
        compiler_params=_cparams(("arbitrary", "arbitrary", "arbitrary")),
        name="rglru",
    )(xb, xb, xb, conv_w, conv_b, wg, bg, c8)


def _gelu_tanh(x):
    return 0.5 * x * (1.0 + jnp.tanh(np.sqrt(2.0 / np.pi).astype(np.float32) * (x + 0.044715 * (x * x * x))))


def _merge_kernel(xc_ref, xl_ref, ya_ref, hf_ref, hb_ref, gb_ref, yc_ref, sg_ref, mod_ref, gpost_ref, gpre_ref,
                  wbr_ref, wout_ref, wr_ref, br_ref, cnt0_ref,
                  x1_ref, h2_ref, idx_ref, wgt_ref, rank_ref, cnt_ref, cnt_s, *, t_off):
    b, ti = pl.program_id(0), pl.program_id(1)
    t = ti + t_off
    d = D_MODEL

    @pl.when((b == 0) & (ti == 0))
    def _init():
        cnt_s[...] = cnt0_ref[...]

    yb = (_gelu_tanh(gb_ref[...]) * (hf_ref[...] + hb_ref[...])).astype(BF16)
    ys = (ya_ref[...], yb, yc_ref[...])
    mix = sg_ref[:, 0:d] * _dot(ys[0], wbr_ref[0])
    for k in range(1, N_BRANCH):
        mix = mix + sg_ref[:, k * d:(k + 1) * d] * _dot(ys[k], wbr_ref[k])
    y = _dot(mix.astype(BF16), wout_ref[...])
    x = jnp.where(t == 0, xc_ref[...], xl_ref[...])
    x1 = x + _mod_row(mod_ref, b, t, 2) * _rms(y, gpost_ref[...])
    x1_ref[...] = x1
    h2 = _rms(x1, gpre_ref[...]) * (1.0 + _mod_row(mod_ref, b, t, 4)) + _mod_row(mod_ref, b, t, 3)
    _store_token_rows(h2_ref, h2, TILE)

    logits = _dot(h2.astype(BF16), wr_ref[...]) + br_ref[...]
    lane = lax.broadcasted_iota(I32, (TILE, LANES), 1)
    work = logits
    tops, sels = [], []
    for k in range(TOP_K):
        m = work.max(axis=-1, keepdims=True)
        first = jnp.where(work == m, lane, LANES).min(axis=-1, keepdims=True)
        sel = lane == first
        work = jnp.where(sel, -3.0e38, work)
        tops.append(m)
        sels.append(sel)
    es = [jnp.exp(m - tops[0]) for m in tops]
    den = es[0] + es[1] + es[2] + es[3]

    chosen = jnp.zeros((TILE, LANES), F32)
    for sel in sels:
        chosen = jnp.where(sel, 1.0, chosen)
    rr = lax.broadcasted_iota(I32, (TILE, TILE), 0)
    cc = lax.broadcasted_iota(I32, (TILE, TILE), 1)
    tri = jnp.where(cc < rr, 1.0, 0.0).astype(BF16)
    pref = _dot(tri, chosen.astype(BF16)) + cnt_s[0:1, :]
    cnt_new = cnt_s[0:1, :] + chosen.sum(axis=0, keepdims=True)
    cnt_s[0:1, :] = cnt_new
    cnt_ref[...] = jnp.broadcast_to(cnt_new, cnt_ref.shape)

    idx_o = jnp.zeros((TILE, LANES), I32)
    wgt_o = jnp.zeros((TILE, LANES), F32)
    rank_o = jnp.zeros((TILE, LANES), I32)
    for k in range(TOP_K):
        first = jnp.where(sels[k], lane, 0).max(axis=-1, keepdims=True)
        rk = jnp.where(sels[k], pref, 0.0).sum(axis=-1, keepdims=True)
        idx_o = jnp.where(lane == k, first, idx_o)
        wgt_o = jnp.where(lane == k, es[k] / den, wgt_o)
        rank_o = jnp.where(lane == k, rk.astype(I32), rank_o)
    idx_ref[...] = idx_o
    wgt_ref[...] = wgt_o
    rank_ref[...] = rank_o


def _merge(seq, ya, hfb, gb, yc, sg, mod, g_post, g_pre, w_br, w_out, w_r, b_r, cnt0, with_ctx):
    bsz, d = seq[0].shape[0], seq[0].shape[-1]
    s = _seq_len(seq)
    t_off = 0 if with_ctx else 1
    nt = s // TILE - t_off
    rows = nt * TILE
    inp = lambda b, t: (b, t + t_off, 0)
    loc = lambda b, t: (b, t, 0)
    c2 = lambda b, t: (0, 0)
    c3 = lambda b, t: (0, 0, 0)

    def out(width, dtype):
        return jax.ShapeDtypeStruct((bsz, rows, width), dtype), pl.BlockSpec((None, TILE, width), loc)

    h2_out = (jax.ShapeDtypeStruct((bsz, rows * TOK_ROWS, LANES), F32),
              pl.BlockSpec((None, TILE * TOK_ROWS, LANES), loc))
    outs = [out(d, F32), h2_out, out(LANES, I32), out(LANES, F32), out(LANES, I32)]
    shapes = [o[0] for o in outs] + [jax.ShapeDtypeStruct((8, LANES), F32)]
    specs = [o[1] for o in outs] + [pl.BlockSpec((8, LANES), c2)]
    return pl.pallas_call(
        functools.partial(_merge_kernel, t_off=t_off),
        grid=(bsz, nt),
        in_specs=_seq_specs(seq, t_off) + [
                  pl.BlockSpec((None, TILE, BRANCH_W), loc),
                  pl.BlockSpec((None, None, TILE, BRANCH_W), lambda b, t: (0, b, t + t_off, 0)),
                  pl.BlockSpec((None, None, TILE, BRANCH_W), lambda b, t: (1, b, t + t_off, 0)),
                  pl.BlockSpec((None, TILE, BRANCH_W), inp),
                  pl.BlockSpec((None, TILE, BRANCH_W), loc),
                  pl.BlockSpec((None, TILE, N_BRANCH * d), inp),
                  pl.BlockSpec((8, 6 * d), c2),
                  pl.BlockSpec((1, d), c2),
                  pl.BlockSpec((1, d), c2),
                  pl.BlockSpec((N_BRANCH, BRANCH_W, d), c3),
                  pl.BlockSpec((d, d), c2),
                  pl.BlockSpec((d, LANES), c2),
                  pl.BlockSpec((1, LANES), c2),
                  pl.BlockSpec((8, LANES), c2)],
        out_specs=specs,
        out_shape=shapes,
        scratch_shapes=[pltpu.VMEM((8, LANES), F32)],
        compiler_params=_cparams(("arbitrary", "arbitrary")),
        name="merge_router",
    )(seq[0], seq[1], ya, hfb, hfb, gb, yc, sg, mod, g_post, g_pre, w_br, w_out, w_r, b_r, cnt0)


def _store_token_rows(ref, val, n, lead=()):
    for j in range(TOK_ROWS):
        ref[lead + (pl.ds(j, n, stride=TOK_ROWS), slice(None))] = val[:, j * LANES:(j + 1) * LANES]


def _load_token_rows(ref, n, lead=()):
    return jnp.concatenate([ref[lead + (pl.ds(j, n, stride=TOK_ROWS), slice(None))] for j in range(TOK_ROWS)], axis=1)


def _token_slice(i):
    return pl.ds(pl.multiple_of(i * TOK_ROWS, TOK_ROWS), TOK_ROWS)


def _dispatch_kernel(pos_ref, last_ref, h_ref, xs_ref, zbuf, sem, zsem):
    step = pl.program_id(0)

    @pl.when(step == 0)
    def _zero_fill():
        zbuf[...] = jnp.zeros_like(zbuf)

        def fill(e):
            dst = xs_ref.at[pl.ds(pl.multiple_of(last_ref[0, e] * TOK_ROWS, TOK_ROWS), MOE_ROWS * TOK_ROWS)]
            return pltpu.make_async_copy(zbuf, dst, zsem)

        for e in range(N_EXPERTS):
            @pl.when(last_ref[0, e] >= 0)
            def _():
                fill(e).start()
        for e in range(N_EXPERTS):
            @pl.when(last_ref[0, e] >= 0)
            def _():
                fill(e).wait()

    def row_copy(i, k):
        p = pos_ref[0, 0, i * TOP_K + k]
        return pltpu.make_async_copy(h_ref.at[_token_slice(i)], xs_ref.at[_token_slice(p)], sem)

    def start(i, c):
        for k in range(TOP_K):
            row_copy(i, k).start(priority=k % 2)
        return c

    lax.fori_loop(0, DISPATCH_TILE, start, 0, unroll=DMA_UNROLL)
    for _ in range(TOP_K):
        pltpu.make_async_copy(h_ref, xs_ref.at[pl.ds(0, DISPATCH_TILE * TOK_ROWS)], sem).wait()


def _dispatch(h2, pos, last_tile_row, n_rows):
    bsz, rows = pos.shape[:2]
    n_steps = bsz * rows // DISPATCH_TILE
    return pl.pallas_call(
        _dispatch_kernel,
        grid=(n_steps,),
        in_specs=[pl.BlockSpec((1, 1, DISPATCH_TILE * TOP_K), lambda i: (i, 0, 0), memory_space=pltpu.SMEM),
                  pl.BlockSpec((1, N_EXPERTS), lambda i: (0, 0), memory_space=pltpu.SMEM),
                  pl.BlockSpec((DISPATCH_TILE * TOK_ROWS, LANES), lambda i: (i, 0))],
        out_specs=pl.BlockSpec(memory_space=pl.ANY),
        out_shape=jax.ShapeDtypeStruct((n_rows * TOK_ROWS, LANES), F32),
        scratch_shapes=[pltpu.VMEM((MOE_ROWS * TOK_ROWS, LANES), F32), pltpu.SemaphoreType.DMA(()),
                        pltpu.SemaphoreType.DMA(())],
        compiler_params=_cparams(("arbitrary",)),
        name="moe_dispatch",
    )(pos.reshape(n_steps, 1, DISPATCH_TILE * TOP_K), last_tile_row, h2.reshape(bsz * rows * TOK_ROWS, LANES))


def _expert_kernel(te_ref, nu_ref, nxt_ref, slot_ref, x_ref, w1_hbm, b1_ref, w2_hbm, b2_ref, y_ref,
                   w1_f, w2_f, w1_s, w2_s, sem, *, layer):
    t = pl.program_id(0)
    prev = te_ref[jnp.maximum(t - 1, 0)]
    active = t < nu_ref[0]

    def fetch(e, sl):
        return (pltpu.make_async_copy(w1_hbm.at[layer, e], w1_f.at[sl], sem.at[0, sl]),
                pltpu.make_async_copy(w2_hbm.at[layer, e], w2_f.at[sl], sem.at[1, sl]))

    @pl.when(t == 0)
    def _first():
        for cp in fetch(te_ref[0], 0):
            cp.start()

    @pl.when(active & ((t == 0) | (te_ref[t] != prev)))
    def _load():
        sl = slot_ref[t]
        for cp in fetch(te_ref[t], sl):
            cp.wait()

        @pl.when(nxt_ref[t] >= 0)
        def _prefetch():
            for cp in fetch(nxt_ref[t], 1 - sl):
                cp.start()

        w1_s[...] = w1_f[sl].astype(BF16)
        w2_s[...] = w2_f[sl].astype(BF16)

    @pl.when(active)
    def _run():
        f = D_EXPERT
        x = _load_token_rows(x_ref, MOE_ROWS).astype(BF16)
        hid = _dot(x, w1_s[...]) + b1_ref[...]
        glu = jnp.minimum(hid[:, :f], SWIGLU_LIMIT)
        lin = jnp.clip(hid[:, f:], -SWIGLU_LIMIT, SWIGLU_LIMIT)
        act = glu * jax.nn.sigmoid(SWIGLU_ALPHA * glu) * (lin + 1.0)
        _store_token_rows(y_ref, _dot(act.astype(BF16), w2_s[...]) + b2_ref[...], MOE_ROWS)

    @pl.when(jnp.logical_not(active))
    def _idle():
        y_ref[...] = jnp.zeros_like(y_ref)


def _experts(xs, tile_expert, n_used, next_expert, slot, w1, b1, w2, b2, layer):
    depth, e, d, f2 = w1.shape
    n_tiles = xs.shape[0] // (MOE_ROWS * TOK_ROWS)
    blk = (MOE_ROWS * TOK_ROWS, LANES)
    xmap = lambda t, te, nu, nx, sl: (jnp.minimum(t, nu[0] - 1), 0)
    bmap = lambda t, te, nu, nx, sl: (layer, te[t], 0, 0)
    grid_spec = pltpu.PrefetchScalarGridSpec(
        num_scalar_prefetch=4,
        grid=(n_tiles,),
        in_specs=[pl.BlockSpec(blk, xmap),
                  pl.BlockSpec(memory_space=pl.ANY),
                  pl.BlockSpec((None, None, 1, f2), bmap),
                  pl.BlockSpec(memory_space=pl.ANY),
                  pl.BlockSpec((None, None, 1, d), bmap)],
        out_specs=pl.BlockSpec(blk, lambda t, te, nu, nx, sl: (t, 0)),
        scratch_shapes=[pltpu.VMEM((2, d, f2), F32), pltpu.VMEM((2, f2 // 2, d), F32),
                        pltpu.VMEM((d, f2), BF16), pltpu.VMEM((f2 // 2, d), BF16),
                        pltpu.SemaphoreType.DMA((2, 2))])
    return pl.pallas_call(
        functools.partial(_expert_kernel, layer=layer),
        grid_spec=grid_spec,
        out_shape=jax.ShapeDtypeStruct(xs.shape, F32),
        compiler_params=_cparams(("arbitrary",)),
        name="moe_experts",
    )(tile_expert, n_used, next_expert, slot, xs, w1, b1.reshape(depth, e, 1, f2), w2, b2.reshape(depth, e, 1, d))


def _combine_kernel(pos_ref, posn_ref, ys_ref, wgt_ref, x1_ref, mod_ref, gpost_ref, o_ref, buf, sem, *, t_off, nt):
    b, ti = pl.program_id(0), pl.program_id(1)
    t = ti + t_off
    step = b * nt + ti
    n_steps = pl.num_programs(0) * nt
    slot = step % 2

    def issue(p_ref, sl):
        def start(i, c):
            for k in range(TOP_K):
                p = p_ref[0, 0, i * TOP_K + k]
                pltpu.make_async_copy(ys_ref.at[_token_slice(p)], buf.at[sl, k, _token_slice(i)],
                                      sem.at[sl]).start(priority=k % 2)
            return c
        lax.fori_loop(0, TILE, start, 0, unroll=DMA_UNROLL)

    @pl.when(step == 0)
    def _():
        issue(pos_ref, 0)

    @pl.when(step + 1 < n_steps)
    def _():
        issue(posn_ref, 1 - slot)

    for k in range(TOP_K):
        pltpu.make_async_copy(ys_ref.at[pl.ds(0, TILE * TOK_ROWS)], buf.at[slot, k], sem.at[slot]).wait()

    wgt = wgt_ref[...]
    moe = wgt[:, 0:1] * _load_token_rows(buf, TILE, (slot, 0))
    for k in range(1, TOP_K):
        moe = moe + wgt[:, k:k + 1] * _load_token_rows(buf, TILE, (slot, k))
    o_ref[...] = x1_ref[...] + _mod_row(mod_ref, b, t, 5) * _rms(moe, gpost_ref[...])


def _combine(ys, pos, wgt, x1, mod, g_post, with_ctx):
    bsz, rows, d = x1.shape
    nt = rows // TILE
    tok = lambda b, t: (b, t, 0)
    last = bsz * nt - 1
    return pl.pallas_call(
        functools.partial(_combine_kernel, t_off=0 if with_ctx else 1, nt=nt),
        grid=(bsz, nt),
        in_specs=[pl.BlockSpec((1, 1, TILE * TOP_K), lambda b, t: (b * nt + t, 0, 0), memory_space=pltpu.SMEM),
                  pl.BlockSpec((1, 1, TILE * TOP_K), lambda b, t: (jnp.minimum(b * nt + t + 1, last), 0, 0),
                               memory_space=pltpu.SMEM),
                  pl.BlockSpec(memory_space=pl.ANY),
                  pl.BlockSpec((None, TILE, LANES), tok),
                  pl.BlockSpec((None, TILE, d), tok),
                  pl.BlockSpec((8, 6 * d), lambda b, t: (0, 0)),
                  pl.BlockSpec((1, d), lambda b, t: (0, 0))],
        out_specs=pl.BlockSpec((None, TILE, d), tok),
        out_shape=jax.ShapeDtypeStruct((bsz, rows, d), F32),
        scratch_shapes=[pltpu.VMEM((2, TOP_K, TILE * TOK_ROWS, LANES), F32), pltpu.SemaphoreType.DMA((2,))],
        compiler_params=_cparams(("arbitrary", "arbitrary")),
        name="moe_combine",
    )(pos.reshape(bsz * nt, 1, TILE * TOP_K), pos.reshape(bsz * nt, 1, TILE * TOP_K), ys, wgt, x1, mod, g_post)


def _moe(h2, idx, wgt, rank, counts, x1, mod, g_post, w1, b1, w2, b2, layer, with_ctx):
    bsz, rows, d = x1.shape
    n_tiles = (bsz * rows * TOP_K) // MOE_ROWS + N_EXPERTS
    n_rows = n_tiles * MOE_ROWS
    cnt = counts.astype(I32)
    tiles_e = (cnt + MOE_ROWS - 1) // MOE_ROWS
    tile_end = jnp.cumsum(tiles_e)
    row_start = (tile_end - tiles_e) * MOE_ROWS
    n_used = tile_end[-1:]
    tile_ids = jnp.minimum(jnp.arange(n_tiles, dtype=I32), n_used[0] - 1)
    tile_expert = jnp.sum((tile_end[None, :] <= tile_ids[:, None]).astype(I32), axis=1)
    tile_expert = jnp.minimum(tile_expert, N_EXPERTS - 1)
    experts = jnp.arange(N_EXPERTS, dtype=I32)
    start_of = jnp.sum(jnp.where(idx[..., :TOP_K, None] == experts, row_start, 0), axis=-1)
    pos = (start_of + rank[..., :TOP_K]).astype(I32)
    last_tile_row = jnp.where(tiles_e > 0, (tile_end - 1) * MOE_ROWS, -1).astype(I32)[None, :]
    xs = _dispatch(h2, pos, last_tile_row, n_rows)
    is_first = jnp.concatenate([jnp.ones((1,), I32), (tile_expert[1:] != tile_expert[:-1]).astype(I32)])
    slot = ((jnp.cumsum(is_first) - 1) % 2).astype(I32)
    group_end = jnp.sum(jnp.where(tile_expert[:, None] == experts, tile_end, 0), axis=1)
    expert_at = jnp.sum(jnp.where(group_end[:, None] == jnp.arange(n_tiles, dtype=I32), tile_expert, 0), axis=1)
    next_expert = jnp.where(group_end < n_used[0], expert_at, -1).astype(I32)
    ys = _experts(xs, tile_expert, n_used.astype(I32), next_expert, slot, w1, b1, w2, b2, layer)
    return _combine(ys, pos, wgt, x1, mod, g_post, with_ctx)


def _dup_heads(w):
    d = w.shape[0]
    w = w.reshape(d, -1, 1, HEAD_DIM)
    return jnp.broadcast_to(w, (d, w.shape[1], 2, HEAD_DIM)).reshape(d, -1)


def _inproj_weight(w_in):
    sizes = (BRANCH_W, BRANCH_W, BRANCH_W, LRU_WIDTH, LRU_WIDTH, BRANCH_W,
             SW_KV_HEADS * HEAD_DIM, SW_KV_HEADS * HEAD_DIM, N_BRANCH * D_MODEL)
    offs = np.cumsum((0,) + sizes)
    qa, ka, va, gb, xb, qs, ks, vs, gl = [w_in[:, offs[i]:offs[i + 1]] for i in range(9)]
    scale = HEAD_DIM ** -0.5
    cols = [qa * scale, ka, va, gb, xb, qs * scale, _dup_heads(ks), _dup_heads(vs), gl]
    return jnp.concatenate(cols, axis=1).astype(BF16)


def _rope_tables(t_lat):
    pos = np.arange(t_lat)
    row = (pos // GRID_W).astype(np.float32)
    col = (pos % GRID_W).astype(np.float32)
    half = ROPE_AXIS_DIM // 2
    inv = (ROPE_BASE ** (-jnp.arange(0, ROPE_AXIS_DIM, 2, dtype=F32) / ROPE_AXIS_DIM))
    d = np.arange(HEAD_DIM)
    axis = d // ROPE_AXIS_DIM
    freq = d % half
    sign = np.where((d % ROPE_AXIS_DIM) < half, -1.0, 1.0).astype(np.float32)
    p = jnp.where(jnp.asarray(axis == 0)[None, :], jnp.asarray(row)[:, None], jnp.asarray(col)[:, None])
    ang = p * inv[freq][None, :]
    cos = jnp.concatenate([jnp.ones((CTX_LEN, HEAD_DIM), F32), jnp.cos(ang)], axis=0)
    sin = jnp.concatenate([jnp.zeros((CTX_LEN, HEAD_DIM), F32), jnp.sin(ang) * sign[None, :]], axis=0)
    return jnp.tile(cos, (1, 2)), jnp.tile(sin, (1, 2))


def _lru_gate_weights(wa, ba, wx, bx):
    def dense(wblk):
        k, bs = wblk.shape[1], wblk.shape[2]
        eye = jnp.eye(k, dtype=wblk.dtype)
        return jnp.einsum("dkij,kl->dkilj", wblk, eye).reshape(2, k * bs, k * bs)
    wg = jnp.concatenate([dense(wa), dense(wx)], axis=-1).astype(BF16)
    bg = jnp.concatenate([ba, bx], axis=-1)[:, None, :].astype(F32)
    return wg, bg


def _layer(seq, mod, g_mix_pre, g_mix_post, g_ffn_pre, g_ffn_post, w_in, na_rpb, conv_w, conv_b,
           lru_wa, lru_ba, lru_wx, lru_bx, lru_lam, sw_sinks, w_branch, w_out, w_router, b_router,
           w1, b1, w2, b2, layer, with_ctx):
    d = seq[0].shape[-1]
    t_lat = _seq_len(seq) - CTX_LEN
    row = lambda v: v.reshape(1, -1).astype(F32)

    cos_t, sin_t = _rope_tables(t_lat)
    qa, ka, va, gb, xb, qs, ks, vs, sg = _inproj(seq, mod, row(g_mix_pre), _inproj_weight(w_in), cos_t, sin_t)

    ya = _na_attention(qa, ka, va, _na_bias_table(na_rpb, t_lat // GRID_W), with_ctx)

    wg, bg = _lru_gate_weights(lru_wa, lru_ba, lru_wx, lru_bx)
    c8 = (-LRU_C * jax.nn.softplus(-lru_lam.astype(F32)))[:, None, :]
    hfb = _lru(xb, conv_w.astype(F32), row(conv_b), wg, bg, c8)

    g = SW_HEADS // SW_KV_HEADS
    sink_col = jnp.repeat(sw_sinks.astype(F32).reshape(SW_KV_HEADS, g), SW_BLOCK, axis=1)[:, :, None]
    yc = _sw_attention(qs, ks, vs, _sw_mask_table(), sink_col, with_ctx)

    w_r = jnp.zeros((d, LANES), F32).at[:, :N_EXPERTS].set(w_router).astype(BF16)
    b_r = jnp.full((1, LANES), NEG_INF, F32).at[0, :N_EXPERTS].set(b_router.astype(F32))
    x1, h2, idx, wgt, rank, cnt = _merge(
        seq, ya, hfb, gb, yc, sg, mod, row(g_mix_post), row(g_ffn_pre), w_branch.astype(BF16),
        w_out.astype(BF16), w_r, b_r, jnp.zeros((8, LANES), F32), with_ctx)

    return _moe(h2, idx, wgt, rank, cnt[0, :N_EXPERTS], x1, mod, row(g_ffn_post), w1, b1, w2, b2, layer, with_ctx)


def kernel(x, c, ctx, c_ctx, w_ada, b_ada, g_mix_pre, g_mix_post, g_ffn_pre, g_ffn_post, w_in, na_rpb, conv_w, conv_b, lru_wa, lru_ba, lru_wx, lru_bx, lru_lam, sw_sinks, w_branch, w_out, w_router, b_router, w1, b1, w2, b2):
    depth = w_ada.shape[0]
    bsz, t_lat, d = x.shape
    assert bsz == 2 and d == D_MODEL and ctx.shape[1] == CTX_LEN
    assert t_lat % TILE == 0 and t_lat // GRID_W >= NA_KROWS and t_lat // SW_BLOCK >= 3
    c8 = jnp.zeros((8, d), F32).at[:bsz].set(c).at[bsz].set(c_ctx)
    mods = _ada(c8, w_ada, b_ada)
    seq = (ctx, x, 0)
    for l in range(depth):
        with_ctx = l < depth - 1
        out = _layer(seq, mods[l], g_mix_pre[l], g_mix_post[l], g_ffn_pre[l], g_ffn_post[l], w_in[l], na_rpb[l],
                     conv_w[l], conv_b[l], lru_wa[l], lru_ba[l], lru_wx[l], lru_bx[l], lru_lam[l], sw_sinks[l],
                     w_branch[l], w_out[l], w_router[l], b_router[l], w1, b1, w2, b2, l, with_ctx)
        seq = (out, out, 1)
    return out
```

```python
import functools

import numpy as np
import jax
import jax.numpy as jnp
from jax import lax
from jax.experimental import pallas as pl
from jax.experimental.pallas import tpu as pltpu

F32 = jnp.float32
BF16 = jnp.bfloat16
I32 = jnp.int32

D_MODEL = 1024
CTX_LEN = 256
GRID_W = 64
HEAD_DIM = 64
BRANCH_W = 512
N_BRANCH = 3
NA_HEADS = 8
NA_WIN_R = 8
NA_WIN_C = 16
LRU_WIDTH = 512
LRU_BLOCKS = 8
LRU_CONV = 4
LRU_C = 8.0
SW_HEADS = 8
SW_KV_HEADS = 2
SW_WINDOW = 128
SW_BLOCK = 128
ROPE_BASE = 10000.0
ROPE_AXIS_DIM = HEAD_DIM // 2
N_EXPERTS = 32
TOP_K = 4
D_EXPERT = 1024
SWIGLU_LIMIT = 7.0
SWIGLU_ALPHA = 1.702
RMS_EPS = 1e-6
NEG_INF = -1e30

LANES = 128
TILE = 256
NA_QROWS = 4
NA_KROWS = 12
NA_PAIRS = 2
MOE_ROWS = 512
DISPATCH_TILE = 512
TOK_ROWS = D_MODEL // LANES
DMA_UNROLL = 8
VMEM_LIMIT = 56 * 1024 * 1024

_C_QA, _C_KA, _C_VA, _C_GB, _C_XB, _C_QS, _C_KS, _C_VS, _C_GL, _C_END = (
    0, 512, 1024, 1536, 2048, 2560, 3072, 3328, 3584, 6656)


def _cparams(sem, vmem=VMEM_LIMIT):
    return pltpu.CompilerParams(dimension_semantics=sem, vmem_limit_bytes=vmem)


def _dot(a, b):
    return jnp.dot(a, b, preferred_element_type=F32)


def _dot_t(a, b):
    return lax.dot_general(a, b, (((1,), (1,)), ((), ())), preferred_element_type=F32)


def _rms(x, g):
    return x * lax.rsqrt(jnp.mean(x * x, axis=-1, keepdims=True) + RMS_EPS) * g


def _ada_kernel(c_ref, w_ref, b_ref, o_ref):
    c = c_ref[...]
    s = c * jax.nn.sigmoid(c)
    o_ref[...] = jnp.dot(s, w_ref[...], preferred_element_type=F32, precision=lax.Precision.HIGHEST) + b_ref[...]


def _ada(c8, w_ada, b_ada):
    depth, d, n = w_ada.shape
    tn = 1536
    return pl.pallas_call(
        _ada_kernel,
        grid=(depth, n // tn),
        in_specs=[pl.BlockSpec((8, d), lambda l, j: (0, 0)),
                  pl.BlockSpec((None, d, tn), lambda l, j: (l, 0, j)),
                  pl.BlockSpec((None, 1, tn), lambda l, j: (l, 0, j))],
        out_specs=pl.BlockSpec((None, 8, tn), lambda l, j: (l, 0, j)),
        out_shape=jax.ShapeDtypeStruct((depth, 8, n), F32),
        compiler_params=_cparams(("arbitrary", "arbitrary")),
        name="ada_mod",
    )(c8, w_ada, b_ada.reshape(depth, 1, n))


def _mod_row(mod_ref, b, t, k):
    row = jnp.where(t == 0, 2, b)
    return mod_ref[pl.ds(row, 1), k * D_MODEL:(k + 1) * D_MODEL]


def _inproj_kernel(xc_ref, xl_ref, mod_ref, g_ref, w_ref, cos_ref, sin_ref,
                   qa_ref, ka_ref, va_ref, gb_ref, xb_ref, qs_ref, ks_ref, vs_ref, sg_ref):
    b, t = pl.program_id(0), pl.program_id(1)
    x = jnp.where(t == 0, xc_ref[...], xl_ref[...])
    h = _rms(x, g_ref[...]) * (1.0 + _mod_row(mod_ref, b, t, 1)) + _mod_row(mod_ref, b, t, 0)
    h = h.astype(BF16)

    def seg(lo, hi):
        return _dot(h, w_ref[:, lo:hi])

    qa_ref[...] = seg(_C_QA, _C_KA).astype(BF16)
    ka_ref[...] = seg(_C_KA, _C_VA).astype(BF16)
    va_ref[...] = seg(_C_VA, _C_GB).astype(BF16)
    gb_ref[...] = seg(_C_GB, _C_XB)
    xb_ref[...] = seg(_C_XB, _C_QS)
    cos = cos_ref[...]
    sin = sin_ref[...]
    half = ROPE_AXIS_DIM // 2
    lane = lax.broadcasted_iota(I32, (1, LANES), 1)
    first_half = (lane % ROPE_AXIS_DIM) < half

    def rope(v):
        partner = jnp.where(first_half, pltpu.roll(v, LANES - half, 1), pltpu.roll(v, half, 1))
        return (v * cos + partner * sin).astype(BF16)

    for j in range(4):
        qs_ref[:, j * LANES:(j + 1) * LANES] = rope(seg(_C_QS + j * LANES, _C_QS + (j + 1) * LANES))
    for j in range(2):
        ks_ref[:, j * LANES:(j + 1) * LANES] = rope(seg(_C_KS + j * LANES, _C_KS + (j + 1) * LANES))
    vs_ref[...] = seg(_C_VS, _C_GL).astype(BF16)
    for j in range(N_BRANCH):
        lo = _C_GL + j * D_MODEL
        sg_ref[:, j * D_MODEL:(j + 1) * D_MODEL] = jax.nn.sigmoid(seg(lo, lo + D_MODEL))


def _seq_specs(seq, t_off=0):
    xc, xl, lat_off = seq
    d = xc.shape[-1]
    return [pl.BlockSpec((None, TILE, d), lambda b, i: (b, 0, 0)),
            pl.BlockSpec((None, TILE, d), lambda b, i: (b, jnp.maximum(i + t_off - 1, 0) + lat_off, 0))]


def _seq_len(seq):
    return CTX_LEN + seq[1].shape[1] - seq[2] * TILE


def _inproj(seq, mod, g_pre, w_cat, cos_t, sin_t):
    bsz, d = seq[0].shape[0], seq[0].shape[-1]
    s = _seq_len(seq)
    nt = s // TILE
    tok = lambda b, t: (b, t, 0)
    const2 = lambda b, t: (0, 0)

    def out(width, dtype):
        return jax.ShapeDtypeStruct((bsz, s, width), dtype), pl.BlockSpec((None, TILE, width), tok)

    outs = [out(512, BF16), out(512, BF16), out(512, BF16), out(512, F32), out(512, F32),
            out(512, BF16), out(256, BF16), out(256, BF16), out(N_BRANCH * D_MODEL, F32)]
    return pl.pallas_call(
        _inproj_kernel,
        grid=(bsz, nt),
        in_specs=_seq_specs(seq) + [
                  pl.BlockSpec((8, 6 * d), const2),
                  pl.BlockSpec((1, d), const2),
                  pl.BlockSpec((d, _C_END), const2),
                  pl.BlockSpec((TILE, LANES), lambda b, t: (t, 0)),
                  pl.BlockSpec((TILE, LANES), lambda b, t: (t, 0))],
        out_specs=[o[1] for o in outs],
        out_shape=[o[0] for o in outs],
        compiler_params=_cparams(("arbitrary", "arbitrary")),
        name="inproj",
    )(seq[0], seq[1], mod, g_pre, w_cat, cos_t, sin_t)


def _two_head_attention(q, score_fn, value_fn):
    lane = lax.broadcasted_iota(I32, (1, LANES), 1)
    low = lane < HEAD_DIM
    outs = []
    for hh in range(2):
        qm = jnp.where(low if hh == 0 else jnp.logical_not(low), q, jnp.zeros_like(q))
        blocks = score_fn(qm, hh)
        m = blocks[0].max(axis=-1, keepdims=True)
        for s in blocks[1:]:
            m = jnp.maximum(m, s.max(axis=-1, keepdims=True))
        ps = [jnp.exp(s - m) for s in blocks]
        l = ps[0].sum(axis=-1, keepdims=True)
        for p in ps[1:]:
            l = l + p.sum(axis=-1, keepdims=True)
        o = value_fn([p.astype(BF16) for p in ps])
        outs.append(o / l)
    return jnp.where(low, outs[0], outs[1])


def _na_kernel(q_ref, k_ref, v_ref, tab_ref, o_ref, *, t_off, t_lat):
    t = pl.program_id(2) + t_off

    for pair in range(NA_PAIRS):
        lanes = slice(pair * LANES, (pair + 1) * LANES)
        q = q_ref[:, lanes]
        kc = k_ref[0:CTX_LEN, lanes]
        vc = v_ref[0:CTX_LEN, lanes]

        if t_off == 0:
            @pl.when(t == 0)
            def _ctx():
                o = _two_head_attention(q, lambda qm, hh: [_dot_t(qm, kc)], lambda ps: _dot(ps[0], vc))
                o_ref[:, lanes] = o.astype(BF16)

        @pl.when(t > 0)
        def _lat():
            nk = NA_KROWS * GRID_W
            start = jnp.clip((t - 2) * TILE, 0, t_lat - nk)
            start = pl.multiple_of(start + CTX_LEN, TILE)
            k = k_ref[pl.ds(start, nk), lanes]
            v = v_ref[pl.ds(start, nk), lanes]
            o = _two_head_attention(
                q,
                lambda qm, hh: [_dot_t(qm, k) + tab_ref[2 * pair + hh], _dot_t(qm, kc)],
                lambda ps: _dot(ps[0], v) + _dot(ps[1], vc))
            o_ref[:, lanes] = o.astype(BF16)


def _na_attention(qa, ka, va, table, with_ctx):
    bsz, s, _ = qa.shape
    t_lat = s - CTX_LEN
    nrb = t_lat // TILE
    t_off = 0 if with_ctx else 1
    steps = nrb + 1 - t_off
    out_rows = steps * TILE

    def var(i):
        rb = jnp.maximum(i + t_off - 1, 0)
        return jnp.where(rb == 0, 0, jnp.where(rb == nrb - 1, 2, 1))

    w = NA_PAIRS * LANES
    return pl.pallas_call(
        functools.partial(_na_kernel, t_off=t_off, t_lat=t_lat),
        grid=(bsz, NA_HEADS // (2 * NA_PAIRS), steps),
        in_specs=[pl.BlockSpec((None, TILE, w), lambda b, p, i: (b, i + t_off, p)),
                  pl.BlockSpec((None, s, w), lambda b, p, i: (b, 0, p)),
                  pl.BlockSpec((None, s, w), lambda b, p, i: (b, 0, p)),
                  pl.BlockSpec((None, 2 * NA_PAIRS, TILE, NA_KROWS * GRID_W), lambda b, p, i: (var(i), p, 0, 0))],
        out_specs=pl.BlockSpec((None, TILE, w), lambda b, p, i: (b, i, p)),
        out_shape=jax.ShapeDtypeStruct((bsz, out_rows, BRANCH_W), BF16),
        compiler_params=_cparams(("arbitrary", "arbitrary", "arbitrary")),
        name="na_attention",
    )(qa, ka, va, table)


def _na_row_geometry(rows):
    qr = np.arange(NA_QROWS)
    kr = np.arange(NA_KROWS)
    r0 = np.array([0, NA_WIN_R // 2, rows - NA_QROWS])
    kr0 = np.array([0, 0, rows - NA_KROWS])
    q_abs = r0[:, None] + qr[None, :]
    k_abs = kr0[:, None] + kr[None, :]
    rs = np.clip(q_abs - NA_WIN_R // 2, 0, rows - NA_WIN_R)
    rvalid = (k_abs[:, None, :] >= rs[:, :, None]) & (k_abs[:, None, :] < rs[:, :, None] + NA_WIN_R)
    ridx = np.clip(k_abs[:, None, :] - q_abs[:, :, None] + NA_WIN_R - 1, 0, 2 * NA_WIN_R - 2)
    return ridx, rvalid


def _na_table_kernel(col_ref, o_ref, *, ridx, rvalid):
    lane = lax.broadcasted_iota(I32, (GRID_W, LANES), 1)
    low = lane < GRID_W
    neg = jnp.full((GRID_W, LANES), NEG_INF, F32)
    for v in range(3):
        @pl.when(pl.program_id(0) == v)
        def _():
            for q in range(NA_QROWS):
                for m in range(NA_KROWS // 2):
                    halves = []
                    for k in (2 * m, 2 * m + 1):
                        halves.append(col_ref[int(ridx[v, q, k])] if rvalid[v, q, k] else neg)
                    o_ref[q * GRID_W:(q + 1) * GRID_W, m * LANES:(m + 1) * LANES] = jnp.where(low, halves[0], halves[1])


def _na_bias_table(rpb, rows):
    ridx, rvalid = _na_row_geometry(rows)
    qc = np.arange(GRID_W)
    kc = np.arange(GRID_W)
    q_start = np.clip(qc - NA_WIN_C // 2, 0, GRID_W - NA_WIN_C)
    cvalid = (kc[None, :] >= q_start[:, None]) & (kc[None, :] < q_start[:, None] + NA_WIN_C)
    cidx = np.clip(kc[None, :] - qc[:, None] + NA_WIN_C - 1, 0, 2 * NA_WIN_C - 2)
    nr, ncol = 2 * NA_WIN_R - 1, 2 * NA_WIN_C - 1
    csel = (np.where(cvalid, cidx, ncol)[..., None] == np.arange(ncol + 1)).astype(np.float32)
    rpb_ext = jnp.concatenate([rpb.astype(F32), jnp.full(rpb.shape[:2] + (1,), NEG_INF, F32)], axis=-1)
    col = jnp.einsum("hij,cdj->hicd", rpb_ext, jnp.asarray(csel), precision=lax.Precision.HIGHEST)
    col = jnp.concatenate([col, col], axis=-1)
    nk = NA_KROWS * GRID_W
    return pl.pallas_call(
        functools.partial(_na_table_kernel, ridx=ridx, rvalid=rvalid),
        grid=(3, NA_HEADS),
        in_specs=[pl.BlockSpec((None, nr, GRID_W, LANES), lambda v, h: (h, 0, 0, 0))],
        out_specs=pl.BlockSpec((None, None, TILE, nk), lambda v, h: (v, h, 0, 0)),
        out_shape=jax.ShapeDtypeStruct((3, NA_HEADS, TILE, nk), F32),
        compiler_params=_cparams(("arbitrary", "arbitrary")),
        name="na_bias_table",
    )(col)


def _sw_kernel(q_ref, k_ref, v_ref, mask_ref, sink_ref, o_ref, *, t_off, t_lat):
    t = pl.program_id(1) + t_off
    lane = lax.broadcasted_iota(I32, (1, LANES), 1)
    low = lane < HEAD_DIM
    zero = jnp.zeros((SW_BLOCK, LANES), BF16)
    n_ctx_tiles = CTX_LEN // SW_BLOCK

    for hk in range(SW_KV_HEADS):
        kv_lanes = slice(hk * LANES, (hk + 1) * LANES)
        q_off = hk * 2 * LANES
        q01 = q_ref[:, q_off:q_off + LANES]
        q23 = q_ref[:, q_off + LANES:q_off + 2 * LANES]
        qs = jnp.concatenate([jnp.where(low, q01, zero), jnp.where(low, zero, q01),
                              jnp.where(low, q23, zero), jnp.where(low, zero, q23)], axis=0)
        kc = k_ref[0:CTX_LEN, kv_lanes]
        vc = v_ref[0:CTX_LEN, kv_lanes]
        sink = sink_ref[hk]
        s_cx = _dot_t(qs, kc)

        def finish(blocks, vals, sink=sink, q_off=q_off):
            m = sink
            for s in blocks:
                m = jnp.maximum(m, s.max(axis=-1, keepdims=True))
            ps = [jnp.exp(s - m) for s in blocks]
            l = jnp.exp(sink - m)
            for p in ps:
                l = l + p.sum(axis=-1, keepdims=True)
            o = _dot(ps[0].astype(BF16), vals[0])
            for p, v in zip(ps[1:], vals[1:]):
                o = o + _dot(p.astype(BF16), v)
            o = o / l
            b = SW_BLOCK
            o_ref[:, q_off:q_off + LANES] = jnp.where(low, o[0:b], o[b:2 * b]).astype(BF16)
            o_ref[:, q_off + LANES:q_off + 2 * LANES] = jnp.where(low, o[2 * b:3 * b], o[3 * b:4 * b]).astype(BF16)

        if t_off == 0:
            @pl.when(t < n_ctx_tiles)
            def _ctx():
                finish([s_cx], [vc])

        @pl.when(t >= n_ctx_tiles)
        def _lat():
            nk = 3 * SW_BLOCK
            start = jnp.clip((t - n_ctx_tiles - 1) * SW_BLOCK, 0, t_lat - nk)
            start = pl.multiple_of(start + CTX_LEN, SW_BLOCK)
            k = k_ref[pl.ds(start, nk), kv_lanes]
            v = v_ref[pl.ds(start, nk), kv_lanes]
            finish([_dot_t(qs, k) + mask_ref[...], s_cx], [v, vc])


def _sw_attention(qs, ks, vs, mask, sink_col, with_ctx):
    bsz, s, _ = qs.shape
    t_lat = s - CTX_LEN
    nb = t_lat // SW_BLOCK
    n_ctx_tiles = CTX_LEN // SW_BLOCK
    t_off = 0 if with_ctx else n_ctx_tiles
    steps = nb + n_ctx_tiles - t_off

    def var(i):
        n = jnp.maximum(i + t_off - n_ctx_tiles, 0)
        return jnp.where(n == 0, 0, jnp.where(n == nb - 1, 2, 1))

    g = SW_HEADS // SW_KV_HEADS
    return pl.pallas_call(
        functools.partial(_sw_kernel, t_off=t_off, t_lat=t_lat),
        grid=(bsz, steps),
        in_specs=[pl.BlockSpec((None, SW_BLOCK, BRANCH_W), lambda b, i: (b, i + t_off, 0)),
                  pl.BlockSpec((None, s, SW_KV_HEADS * LANES), lambda b, i: (b, 0, 0)),
                  pl.BlockSpec((None, s, SW_KV_HEADS * LANES), lambda b, i: (b, 0, 0)),
                  pl.BlockSpec((None, g * SW_BLOCK, 3 * SW_BLOCK), lambda b, i: (var(i), 0, 0)),
                  pl.BlockSpec((SW_KV_HEADS, g * SW_BLOCK, 1), lambda b, i: (0, 0, 0))],
        out_specs=pl.BlockSpec((None, SW_BLOCK, BRANCH_W), lambda b, i: (b, i, 0)),
        out_shape=jax.ShapeDtypeStruct((bsz, steps * SW_BLOCK, BRANCH_W), BF16),
        compiler_params=_cparams(("arbitrary", "arbitrary")),
        name="sw_attention",
    )(qs, ks, vs, mask, sink_col)


def _sw_mask_table():
    i = np.arange(SW_BLOCK)[:, None]
    j = np.arange(3 * SW_BLOCK)[None, :]
    tabs = []
    for shift in (0, SW_BLOCK, 2 * SW_BLOCK):
        rel = j - i - shift
        tabs.append(np.where(np.abs(rel) <= SW_WINDOW, 0.0, NEG_INF).astype(np.float32))
    tab = np.stack(tabs)
    return jnp.asarray(np.tile(tab, (1, SW_HEADS // SW_KV_HEADS, 1)))


def _lru_kernel(x_ref, xp_ref, xn_ref, cw_ref, cb_ref, wg_ref, bg_ref, c8_ref, o_ref,
                xs, carry, *, nc):
    d, i = pl.program_id(1), pl.program_id(2)
    ci = jnp.where(d == 0, i, jnp.where(i == 0, 0, nc - i))
    tc = TILE
    w = LRU_WIDTH

    no_prev = (ci == 0) | (ci == 1)
    no_next = (ci == 0) | (ci == nc - 1)
    xs[0:8, :] = jnp.where(no_prev, 0.0, xp_ref[...])
    xs[8:8 + tc, :] = x_ref[...]
    xs[8 + tc:16 + tc, :] = jnp.where(no_next, 0.0, xn_ref[...])
    u = cb_ref[...] + cw_ref[0:1, :] * xs[7:7 + tc, :]
    for j in range(1, LRU_CONV):
        u = u + cw_ref[j:j + 1, :] * xs[7 + j:7 + j + tc, :]

    g = _dot(u.astype(BF16), wg_ref[...]) + bg_ref[...]
    r = jax.nn.sigmoid(g[:, :w])
    ig = jax.nn.sigmoid(g[:, w:])
    log_a = c8_ref[...] * r
    a = jnp.exp(log_a)
    th = jnp.tanh(log_a)
    bb = jnp.sqrt(-2.0 * th / (1.0 - th)) * (ig * u)
    @pl.when(i == 0)
    def _init():
        carry[...] = jnp.zeros_like(carry)

    ng = tc // 8
    a3 = a.reshape(ng, 8, w)
    b3 = bb.reshape(ng, 8, w)
    sub = lax.broadcasted_iota(I32, (ng, 8, w), 1)

    def scan(reverse):
        ag, bg_ = a3, b3
        for s in (1, 2, 4):
            shift = 8 - s if reverse else s
            keep = (sub < 8 - s) if reverse else (sub >= s)
            a_sh = pltpu.roll(ag, shift, 1)
            b_sh = pltpu.roll(bg_, shift, 1)
            bg_ = jnp.where(keep, ag * b_sh + bg_, bg_)
            ag = jnp.where(keep, ag * a_sh, ag)
        h = carry[0:1, :]
        for gi in (range(ng - 1, -1, -1) if reverse else range(ng)):
            hg = bg_[gi] + ag[gi] * h
            o_ref[gi * 8:(gi + 1) * 8, :] = hg
            h = hg[0:1, :] if reverse else hg[7:8, :]
        carry[0:1, :] = h

    @pl.when(d == 0)
    def _fwd():
        scan(False)

    @pl.when(d == 1)
    def _bwd():
        scan(True)


def _lru(xb, conv_w, conv_b, wg, bg, c8):
    bsz, s, w = xb.shape
    nc = s // TILE
    r8 = TILE // 8

    def chunk(d, i):
        return jnp.where(d == 0, i, jnp.where(i == 0, 0, nc - i))

    return pl.pallas_call(
        functools.partial(_lru_kernel, nc=nc),
        grid=(bsz, 2, nc),
        in_specs=[pl.BlockSpec((None, TILE, w), lambda b, d, i: (b, chunk(d, i), 0)),
                  pl.BlockSpec((None, 8, w), lambda b, d, i: (b, jnp.maximum(chunk(d, i) * r8 - 1, 0), 0)),
                  pl.BlockSpec((None, 8, w), lambda b, d, i: (b, jnp.minimum((chunk(d, i) + 1) * r8, s // 8 - 1), 0)),
                  pl.BlockSpec((LRU_CONV, w), lambda b, d, i: (0, 0)),
                  pl.BlockSpec((1, w), lambda b, d, i: (0, 0)),
                  pl.BlockSpec((None, w, 2 * w), lambda b, d, i: (d, 0, 0)),
                  pl.BlockSpec((None, 1, 2 * w), lambda b, d, i: (d, 0, 0)),
                  pl.BlockSpec((None, 1, w), lambda b, d, i: (d, 0, 0))],
        out_specs=pl.BlockSpec((None, None, TILE, w), lambda b, d, i: (d, b, chunk(d, i), 0)),
        out_shape=jax.ShapeDtypeStruct((2, bsz, s, w), F32),
        scratch_shapes=[pltpu.VMEM((TILE + 16, w), F32), pltpu.VMEM((8, w), F32)],
        compiler_params=_cparams(("arbitrary", "arbitrary", "arbitrary")),
        name="rglru",
    )(xb, xb, xb, conv_w, conv_b, wg, bg, c8)


def _gelu_tanh(x):
    return 0.5 * x * (1.0 + jnp.tanh(np.sqrt(2.0 / np.pi).astype(np.float32) * (x + 0.044715 * (x * x * x))))


def _merge_kernel(xc_ref, xl_ref, ya_ref, hf_ref, hb_ref, gb_ref, yc_ref, sg_ref, mod_ref, gpost_ref, gpre_ref,
                  wbr_ref, wout_ref, wr_ref, br_ref, cnt0_ref,
                  x1_ref, h2_ref, idx_ref, wgt_ref, rank_ref, cnt_ref, cnt_s, *, t_off):
    b, ti = pl.program_id(0), pl.program_id(1)
    t = ti + t_off
    d = D_MODEL

    @pl.when((b == 0) & (ti == 0))
    def _init():
        cnt_s[...] = cnt0_ref[...]

    yb = (_gelu_tanh(gb_ref[...]) * (hf_ref[...] + hb_ref[...])).astype(BF16)
    ys = (ya_ref[...], yb, yc_ref[...])
    mix = sg_ref[:, 0:d] * _dot(ys[0], wbr_ref[0])
    for k in range(1, N_BRANCH):
        mix = mix + sg_ref[:, k * d:(k + 1) * d] * _dot(ys[k], wbr_ref[k])
    y = _dot(mix.astype(BF16), wout_ref[...])
    x = jnp.where(t == 0, xc_ref[...], xl_ref[...])
    x1 = x + _mod_row(mod_ref, b, t, 2) * _rms(y, gpost_ref[...])
    x1_ref[...] = x1
    h2 = _rms(x1, gpre_ref[...]) * (1.0 + _mod_row(mod_ref, b, t, 4)) + _mod_row(mod_ref, b, t, 3)
    _store_token_rows(h2_ref, h2, TILE)

    logits = _dot(h2.astype(BF16), wr_ref[...]) + br_ref[...]
    lane = lax.broadcasted_iota(I32, (TILE, LANES), 1)
    lane_f = lane.astype(F32)
    work = logits
    tops, sels, firsts = [], [], []
    for k in range(TOP_K):
        m = work.max(axis=-1, keepdims=True)
        first = jnp.where(work == m, lane_f, float(LANES)).min(axis=-1, keepdims=True)
        sel = lane_f == first
        work = jnp.where(sel, -3.0e38, work)
        tops.append(m)
        sels.append(sel)
        firsts.append(first)
    es = [jnp.exp(m - tops[0]) for m in tops]
    den = es[0] + es[1] + es[2] + es[3]

    chosen = jnp.zeros((TILE, LANES), F32)
    for sel in sels:
        chosen = jnp.where(sel, 1.0, chosen)
    rr = lax.broadcasted_iota(I32, (TILE, TILE), 0)
    cc = lax.broadcasted_iota(I32, (TILE, TILE), 1)
    tri = jnp.where(cc < rr, 1.0, 0.0).astype(BF16)
    pref = _dot(tri, chosen.astype(BF16)) + cnt_s[0:1, :]
    cnt_new = cnt_s[0:1, :] + chosen.sum(axis=0, keepdims=True)
    cnt_s[0:1, :] = cnt_new
    cnt_ref[...] = jnp.broadcast_to(cnt_new, cnt_ref.shape)

    idx_o = jnp.zeros((TILE, LANES), I32)
    wgt_o = jnp.zeros((TILE, LANES), F32)
    rank_o = jnp.zeros((TILE, LANES), I32)
    for k in range(TOP_K):
        rk = jnp.where(sels[k], pref, 0.0).sum(axis=-1, keepdims=True)
        idx_o = jnp.where(lane == k, firsts[k].astype(I32), idx_o)
        wgt_o = jnp.where(lane == k, es[k] / den, wgt_o)
        rank_o = jnp.where(lane == k, rk.astype(I32), rank_o)
    idx_ref[...] = idx_o
    wgt_ref[...] = wgt_o
    rank_ref[...] = rank_o


def _merge(seq, ya, hfb, gb, yc, sg, mod, g_post, g_pre, w_br, w_out, w_r, b_r, cnt0, with_ctx):
    bsz, d = seq[0].shape[0], seq[0].shape[-1]
    s = _seq_len(seq)
    t_off = 0 if with_ctx else 1
    nt = s // TILE - t_off
    rows = nt * TILE
    inp = lambda b, t: (b, t + t_off, 0)
    loc = lambda b, t: (b, t, 0)
    c2 = lambda b, t: (0, 0)
    c3 = lambda b, t: (0, 0, 0)

    def out(width, dtype):
        return jax.ShapeDtypeStruct((bsz, rows, width), dtype), pl.BlockSpec((None, TILE, width), loc)

    h2_out = (jax.ShapeDtypeStruct((bsz, rows * TOK_ROWS, LANES), F32),
              pl.BlockSpec((None, TILE * TOK_ROWS, LANES), loc))
    outs = [out(d, F32), h2_out, out(LANES, I32), out(LANES, F32), out(LANES, I32)]
    shapes = [o[0] for o in outs] + [jax.ShapeDtypeStruct((8, LANES), F32)]
    specs = [o[1] for o in outs] + [pl.BlockSpec((8, LANES), c2)]
    return pl.pallas_call(
        functools.partial(_merge_kernel, t_off=t_off),
        grid=(bsz, nt),
        in_specs=_seq_specs(seq, t_off) + [
                  pl.BlockSpec((None, TILE, BRANCH_W), loc),
                  pl.BlockSpec((None, None, TILE, BRANCH_W), lambda b, t: (0, b, t + t_off, 0)),
                  pl.BlockSpec((None, None, TILE, BRANCH_W), lambda b, t: (1, b, t + t_off, 0)),
                  pl.BlockSpec((None, TILE, BRANCH_W), inp),
                  pl.BlockSpec((None, TILE, BRANCH_W), loc),
                  pl.BlockSpec((None, TILE, N_BRANCH * d), inp),
                  pl.BlockSpec((8, 6 * d), c2),
                  pl.BlockSpec((1, d), c2),
                  pl.BlockSpec((1, d), c2),
                  pl.BlockSpec((N_BRANCH, BRANCH_W, d), c3),
                  pl.BlockSpec((d, d), c2),
                  pl.BlockSpec((d, LANES), c2),
                  pl.BlockSpec((1, LANES), c2),
                  pl.BlockSpec((8, LANES), c2)],
        out_specs=specs,
        out_shape=shapes,
        scratch_shapes=[pltpu.VMEM((8, LANES), F32)],
        compiler_params=_cparams(("arbitrary", "arbitrary")),
        name="merge_router",
    )(seq[0], seq[1], ya, hfb, hfb, gb, yc, sg, mod, g_post, g_pre, w_br, w_out, w_r, b_r, cnt0)


def _store_token_rows(ref, val, n, lead=()):
    for j in range(TOK_ROWS):
        ref[lead + (pl.ds(j, n, stride=TOK_ROWS), slice(None))] = val[:, j * LANES:(j + 1) * LANES]


def _load_token_rows(ref, n, lead=()):
    return jnp.concatenate([ref[lead + (pl.ds(j, n, stride=TOK_ROWS), slice(None))] for j in range(TOK_ROWS)], axis=1)


def _token_slice(i):
    return pl.ds(pl.multiple_of(i * TOK_ROWS, TOK_ROWS), TOK_ROWS)


def _dispatch_kernel(pos_ref, last_ref, h_ref, xs_ref, zbuf, sem, zsem):
    step = pl.program_id(0)

    @pl.when(step == 0)
    def _zero_fill():
        zbuf[...] = jnp.zeros_like(zbuf)

        def fill(e):
            dst = xs_ref.at[pl.ds(pl.multiple_of(last_ref[0, e] * TOK_ROWS, TOK_ROWS), MOE_ROWS * TOK_ROWS)]
            return pltpu.make_async_copy(zbuf, dst, zsem)

        for e in range(N_EXPERTS):
            @pl.when(last_ref[0, e] >= 0)
            def _():
                fill(e).start()
        for e in range(N_EXPERTS):
            @pl.when(last_ref[0, e] >= 0)
            def _():
                fill(e).wait()

    def row_copy(i, k):
        p = pos_ref[0, 0, i * TOP_K + k]
        return pltpu.make_async_copy(h_ref.at[_token_slice(i)], xs_ref.at[_token_slice(p)], sem)

    def start(i, c):
        for k in range(TOP_K):
            row_copy(i, k).start(priority=k % 2)
        return c

    lax.fori_loop(0, DISPATCH_TILE, start, 0, unroll=DMA_UNROLL)
    for _ in range(TOP_K):
        pltpu.make_async_copy(h_ref, xs_ref.at[pl.ds(0, DISPATCH_TILE * TOK_ROWS)], sem).wait()


def _dispatch(h2, pos, last_tile_row, n_rows):
    bsz, rows = pos.shape[:2]
    n_steps = bsz * rows // DISPATCH_TILE
    return pl.pallas_call(
        _dispatch_kernel,
        grid=(n_steps,),
        in_specs=[pl.BlockSpec((1, 1, DISPATCH_TILE * TOP_K), lambda i: (i, 0, 0), memory_space=pltpu.SMEM),
                  pl.BlockSpec((1, N_EXPERTS), lambda i: (0, 0), memory_space=pltpu.SMEM),
                  pl.BlockSpec((DISPATCH_TILE * TOK_ROWS, LANES), lambda i: (i, 0))],
        out_specs=pl.BlockSpec(memory_space=pl.ANY),
        out_shape=jax.ShapeDtypeStruct((n_rows * TOK_ROWS, LANES), F32),
        scratch_shapes=[pltpu.VMEM((MOE_ROWS * TOK_ROWS, LANES), F32), pltpu.SemaphoreType.DMA(()),
                        pltpu.SemaphoreType.DMA(())],
        compiler_params=_cparams(("arbitrary",)),
        name="moe_dispatch",
    )(pos.reshape(n_steps, 1, DISPATCH_TILE * TOP_K), last_tile_row, h2.reshape(bsz * rows * TOK_ROWS, LANES))


def _expert_kernel(te_ref, nu_ref, nxt_ref, slot_ref, x_ref, w1_hbm, b1_ref, w2_hbm, b2_ref, y_ref,
                   w1_f, w2_f, w1_s, w2_s, sem, *, layer):
    t = pl.program_id(0)
    prev = te_ref[jnp.maximum(t - 1, 0)]
    active = t < nu_ref[0]

    def fetch(e, sl):
        return (pltpu.make_async_copy(w1_hbm.at[layer, e], w1_f.at[sl], sem.at[0, sl]),
                pltpu.make_async_copy(w2_hbm.at[layer, e], w2_f.at[sl], sem.at[1, sl]))

    @pl.when(t == 0)
    def _first():
        for cp in fetch(te_ref[0], 0):
            cp.start()

    @pl.when(active & ((t == 0) | (te_ref[t] != prev)))
    def _load():
        sl = slot_ref[t]
        for cp in fetch(te_ref[t], sl):
            cp.wait()

        @pl.when(nxt_ref[t] >= 0)
        def _prefetch():
            for cp in fetch(nxt_ref[t], 1 - sl):
                cp.start()

        w1_s[...] = w1_f[sl].astype(BF16)
        w2_s[...] = w2_f[sl].astype(BF16)

    @pl.when(active)
    def _run():
        f = D_EXPERT
        x = _load_token_rows(x_ref, MOE_ROWS).astype(BF16)
        hid = _dot(x, w1_s[...]) + b1_ref[...]
        glu = jnp.minimum(hid[:, :f], SWIGLU_LIMIT)
        lin = jnp.clip(hid[:, f:], -SWIGLU_LIMIT, SWIGLU_LIMIT)
        act = glu * jax.nn.sigmoid(SWIGLU_ALPHA * glu) * (lin + 1.0)
        _store_token_rows(y_ref, _dot(act.astype(BF16), w2_s[...]) + b2_ref[...], MOE_ROWS)

    @pl.when(jnp.logical_not(active))
    def _idle():
        y_ref[...] = jnp.zeros_like(y_ref)


def _experts(xs, tile_expert, n_used, next_expert, slot, w1, b1, w2, b2, layer):
    depth, e, d, f2 = w1.shape
    n_tiles = xs.shape[0] // (MOE_ROWS * TOK_ROWS)
    blk = (MOE_ROWS * TOK_ROWS, LANES)
    xmap = lambda t, te, nu, nx, sl: (jnp.minimum(t, nu[0] - 1), 0)
    bmap = lambda t, te, nu, nx, sl: (layer, te[t], 0, 0)
    grid_spec = pltpu.PrefetchScalarGridSpec(
        num_scalar_prefetch=4,
        grid=(n_tiles,),
        in_specs=[pl.BlockSpec(blk, xmap),
                  pl.BlockSpec(memory_space=pl.ANY),
                  pl.BlockSpec((None, None, 1, f2), bmap),
                  pl.BlockSpec(memory_space=pl.ANY),
                  pl.BlockSpec((None, None, 1, d), bmap)],
        out_specs=pl.BlockSpec(blk, lambda t, te, nu, nx, sl: (t, 0)),
        scratch_shapes=[pltpu.VMEM((2, d, f2), F32), pltpu.VMEM((2, f2 // 2, d), F32),
                        pltpu.VMEM((d, f2), BF16), pltpu.VMEM((f2 // 2, d), BF16),
                        pltpu.SemaphoreType.DMA((2, 2))])
    return pl.pallas_call(
        functools.partial(_expert_kernel, layer=layer),
        grid_spec=grid_spec,
        out_shape=jax.ShapeDtypeStruct(xs.shape, F32),
        compiler_params=_cparams(("arbitrary",)),
        name="moe_experts",
    )(tile_expert, n_used, next_expert, slot, xs, w1, b1.reshape(depth, e, 1, f2), w2, b2.reshape(depth, e, 1, d))


def _combine_kernel(pos_ref, posn_ref, ys_ref, wgt_ref, x1_ref, mod_ref, gpost_ref, o_ref, buf, sem, *, t_off, nt):
    b, ti = pl.program_id(0), pl.program_id(1)
    t = ti + t_off
    step = b * nt + ti
    n_steps = pl.num_programs(0) * nt
    slot = step % 2

    def issue(p_ref, sl):
        def start(i, c):
            for k in range(TOP_K):
                p = p_ref[0, 0, i * TOP_K + k]
                pltpu.make_async_copy(ys_ref.at[_token_slice(p)], buf.at[sl, k, _token_slice(i)],
                                      sem.at[sl]).start(priority=k % 2)
            return c
        lax.fori_loop(0, TILE, start, 0, unroll=DMA_UNROLL)

    @pl.when(step == 0)
    def _():
        issue(pos_ref, 0)

    @pl.when(step + 1 < n_steps)
    def _():
        issue(posn_ref, 1 - slot)

    for k in range(TOP_K):
        pltpu.make_async_copy(ys_ref.at[pl.ds(0, TILE * TOK_ROWS)], buf.at[slot, k], sem.at[slot]).wait()

    wgt = wgt_ref[...]
    moe = wgt[:, 0:1] * _load_token_rows(buf, TILE, (slot, 0))
    for k in range(1, TOP_K):
        moe = moe + wgt[:, k:k + 1] * _load_token_rows(buf, TILE, (slot, k))
    o_ref[...] = x1_ref[...] + _mod_row(mod_ref, b, t, 5) * _rms(moe, gpost_ref[...])


def _combine(ys, pos, wgt, x1, mod, g_post, with_ctx):
    bsz, rows, d = x1.shape
    nt = rows // TILE
    tok = lambda b, t: (b, t, 0)
    last = bsz * nt - 1
    return pl.pallas_call(
        functools.partial(_combine_kernel, t_off=0 if with_ctx else 1, nt=nt),
        grid=(bsz, nt),
        in_specs=[pl.BlockSpec((1, 1, TILE * TOP_K), lambda b, t: (b * nt + t, 0, 0), memory_space=pltpu.SMEM),
                  pl.BlockSpec((1, 1, TILE * TOP_K), lambda b, t: (jnp.minimum(b * nt + t + 1, last), 0, 0),
                               memory_space=pltpu.SMEM),
                  pl.BlockSpec(memory_space=pl.ANY),
                  pl.BlockSpec((None, TILE, LANES), tok),
                  pl.BlockSpec((None, TILE, d), tok),
                  pl.BlockSpec((8, 6 * d), lambda b, t: (0, 0)),
                  pl.BlockSpec((1, d), lambda b, t: (0, 0))],
        out_specs=pl.BlockSpec((None, TILE, d), tok),
        out_shape=jax.ShapeDtypeStruct((bsz, rows, d), F32),
        scratch_shapes=[pltpu.VMEM((2, TOP_K, TILE * TOK_ROWS, LANES), F32), pltpu.SemaphoreType.DMA((2,))],
        compiler_params=_cparams(("arbitrary", "arbitrary")),
        name="moe_combine",
    )(pos.reshape(bsz * nt, 1, TILE * TOP_K), pos.reshape(bsz * nt, 1, TILE * TOP_K), ys, wgt, x1, mod, g_post)


def _moe(h2, idx, wgt, rank, counts, x1, mod, g_post, w1, b1, w2, b2, layer, with_ctx):
    bsz, rows, d = x1.shape
    n_tiles = (bsz * rows * TOP_K) // MOE_ROWS + N_EXPERTS
    n_rows = n_tiles * MOE_ROWS
    cnt = counts.astype(I32)
    tiles_e = (cnt + MOE_ROWS - 1) // MOE_ROWS
    tile_end = jnp.cumsum(tiles_e)
    row_start = (tile_end - tiles_e) * MOE_ROWS
    n_used = tile_end[-1:]
    tile_ids = jnp.minimum(jnp.arange(n_tiles, dtype=I32), n_used[0] - 1)
    tile_expert = jnp.sum((tile_end[None, :] <= tile_ids[:, None]).astype(I32), axis=1)
    tile_expert = jnp.minimum(tile_expert, N_EXPERTS - 1)
    experts = jnp.arange(N_EXPERTS, dtype=I32)
    start_of = jnp.sum(jnp.where(idx[..., :TOP_K, None] == experts, row_start, 0), axis=-1)
    pos = (start_of + rank[..., :TOP_K]).astype(I32)
    last_tile_row = jnp.where(tiles_e > 0, (tile_end - 1) * MOE_ROWS, -1).astype(I32)[None, :]
    xs = _dispatch(h2, pos, last_tile_row, n_rows)
    is_first = jnp.concatenate([jnp.ones((1,), I32), (tile_expert[1:] != tile_expert[:-1]).astype(I32)])
    slot = ((jnp.cumsum(is_first) - 1) % 2).astype(I32)
    group_end = jnp.sum(jnp.where(tile_expert[:, None] == experts, tile_end, 0), axis=1)
    expert_at = jnp.sum(jnp.where(group_end[:, None] == jnp.arange(n_tiles, dtype=I32), tile_expert, 0), axis=1)
    next_expert = jnp.where(group_end < n_used[0], expert_at, -1).astype(I32)
    ys = _experts(xs, tile_expert, n_used.astype(I32), next_expert, slot, w1, b1, w2, b2, layer)
    return _combine(ys, pos, wgt, x1, mod, g_post, with_ctx)


def _dup_heads(w):
    d = w.shape[0]
    w = w.reshape(d, -1, 1, HEAD_DIM)
    return jnp.broadcast_to(w, (d, w.shape[1], 2, HEAD_DIM)).reshape(d, -1)


def _inproj_weight(w_in):
    sizes = (BRANCH_W, BRANCH_W, BRANCH_W, LRU_WIDTH, LRU_WIDTH, BRANCH_W,
             SW_KV_HEADS * HEAD_DIM, SW_KV_HEADS * HEAD_DIM, N_BRANCH * D_MODEL)
    offs = np.cumsum((0,) + sizes)
    qa, ka, va, gb, xb, qs, ks, vs, gl = [w_in[:, offs[i]:offs[i + 1]] for i in range(9)]
    scale = HEAD_DIM ** -0.5
    cols = [qa * scale, ka, va, gb, xb, qs * scale, _dup_heads(ks), _dup_heads(vs), gl]
    return jnp.concatenate(cols, axis=1).astype(BF16)


def _rope_tables(t_lat):
    pos = np.arange(t_lat)
    row = (pos // GRID_W).astype(np.float32)
    col = (pos % GRID_W).astype(np.float32)
    half = ROPE_AXIS_DIM // 2
    inv = (ROPE_BASE ** (-jnp.arange(0, ROPE_AXIS_DIM, 2, dtype=F32) / ROPE_AXIS_DIM))
    d = np.arange(HEAD_DIM)
    axis = d // ROPE_AXIS_DIM
    freq = d % half
    sign = np.where((d % ROPE_AXIS_DIM) < half, -1.0, 1.0).astype(np.float32)
    p = jnp.where(jnp.asarray(axis == 0)[None, :], jnp.asarray(row)[:, None], jnp.asarray(col)[:, None])
    ang = p * inv[freq][None, :]
    cos = jnp.concatenate([jnp.ones((CTX_LEN, HEAD_DIM), F32), jnp.cos(ang)], axis=0)
    sin = jnp.concatenate([jnp.zeros((CTX_LEN, HEAD_DIM), F32), jnp.sin(ang) * sign[None, :]], axis=0)
    return jnp.tile(cos, (1, 2)), jnp.tile(sin, (1, 2))


def _lru_gate_weights(wa, ba, wx, bx):
    def dense(wblk):
        k, bs = wblk.shape[1], wblk.shape[2]
        eye = jnp.eye(k, dtype=wblk.dtype)
        return jnp.einsum("dkij,kl->dkilj", wblk, eye).reshape(2, k * bs, k * bs)
    wg = jnp.concatenate([dense(wa), dense(wx)], axis=-1).astype(BF16)
    bg = jnp.concatenate([ba, bx], axis=-1)[:, None, :].astype(F32)
    return wg, bg


def _layer(seq, mod, g_mix_pre, g_mix_post, g_ffn_pre, g_ffn_post, w_in, na_rpb, conv_w, conv_b,
           lru_wa, lru_ba, lru_wx, lru_bx, lru_lam, sw_sinks, w_branch, w_out, w_router, b_router,
           w1, b1, w2, b2, layer, with_ctx):
    d = seq[0].shape[-1]
    t_lat = _seq_len(seq) - CTX_LEN
    row = lambda v: v.reshape(1, -1).astype(F32)

    cos_t, sin_t = _rope_tables(t_lat)
    qa, ka, va, gb, xb, qs, ks, vs, sg = _inproj(seq, mod, row(g_mix_pre), _inproj_weight(w_in), cos_t, sin_t)

    ya = _na_attention(qa, ka, va, _na_bias_table(na_rpb, t_lat // GRID_W), with_ctx)

    wg, bg = _lru_gate_weights(lru_wa, lru_ba, lru_wx, lru_bx)
    c8 = (-LRU_C * jax.nn.softplus(-lru_lam.astype(F32)))[:, None, :]
    hfb = _lru(xb, conv_w.astype(F32), row(conv_b), wg, bg, c8)

    g = SW_HEADS // SW_KV_HEADS
    sink_col = jnp.repeat(sw_sinks.astype(F32).reshape(SW_KV_HEADS, g), SW_BLOCK, axis=1)[:, :, None]
    yc = _sw_attention(qs, ks, vs, _sw_mask_table(), sink_col, with_ctx)

    w_r = jnp.zeros((d, LANES), F32).at[:, :N_EXPERTS].set(w_router).astype(BF16)
    b_r = jnp.full((1, LANES), NEG_INF, F32).at[0, :N_EXPERTS].set(b_router.astype(F32))
    x1, h2, idx, wgt, rank, cnt = _merge(
        seq, ya, hfb, gb, yc, sg, mod, row(g_mix_post), row(g_ffn_pre), w_branch.astype(BF16),
        w_out.astype(BF16), w_r, b_r, jnp.zeros((8, LANES), F32), with_ctx)

    return _moe(h2, idx, wgt, rank, cnt[0, :N_EXPERTS], x1, mod, row(g_ffn_post), w1, b1, w2, b2, layer, with_ctx)


def kernel(x, c, ctx, c_ctx, w_ada, b_ada, g_mix_pre, g_mix_post, g_ffn_pre, g_ffn_post, w_in, na_rpb, conv_w, conv_b, lru_wa, lru_ba, lru_wx, lru_bx, lru_lam, sw_sinks, w_branch, w_out, w_router, b_router, w1, b1, w2, b2):
    depth = w_ada.shape[0]
    bsz, t_lat, d = x.shape
    assert bsz == 2 and d == D_MODEL and ctx.shape[1] == CTX_LEN
    assert t_lat % TILE == 0 and t_lat // GRID_W >= NA_KROWS and t_lat // SW_BLOCK >= 3
    c8 = jnp.zeros((8, d), F32).at[:bsz].set(c).at[bsz].set(c_ctx)
    mods = _ada(c8, w_ada, b_ada)
    seq = (ctx, x, 0)
    for l in range(depth):
        with_ctx = l < depth - 1
        out = _layer(seq, mods[l], g_mix_pre[l], g_mix_post[l], g_ffn_pre[l], g_ffn_post[l], w_in[l], na_rpb[l],
                     conv_w[l], conv_b[l], lru_wa[l], lru_ba[l], lru_wx[l], lru_bx[l], lru_lam[l], sw_sinks[l],
                     w_branch[l], w_out[l], w_router[l], b_router[l], w1, b1, w2, b2, l, with_ctx)
        seq = (out, out, 1)
    return out
```
